```python
import math
import jax, jax.numpy as jnp
from jax import lax
import numpy as np

D_MODEL = 1024
BATCH = 2
SEQ = 8192
DEPTH = 4
DEC_BATCH = 128
DEC_SEQ = 1
PAST_LEN = 2048
PAGE_SIZE = 128

MIX_W = D_MODEL
HEAD_DIM = 64
VAL_DIM = 2 * HEAD_DIM
N_HEADS = MIX_W // VAL_DIM
CONV_K = 3
N_XHEADS = 4
XHEAD_DIM = 128
XATTN_W = N_XHEADS * XHEAD_DIM
N_MEM = 256
IN_W = 3 * MIX_W + XATTN_W
OUT_W = MIX_W + XATTN_W
D_FF = 2816
FFN_K = 3
N_BUCKETS = 32
MAX_DISTANCE = 128
Q_BLOCK = 128
N_ATTN_LAYERS = (DEPTH + 1) // 2
N_CONV_LAYERS = DEPTH // 2
EPS = 1e-6

kernel_name = "hybrid_diffattn_shortconv_decoder_step"


def rmsnorm(x, g):
    xf = x.astype(jnp.float32)
    y = xf * lax.rsqrt(jnp.mean(xf * xf, axis=-1, keepdims=True) + EPS)
    return (y * g.astype(jnp.float32)).astype(x.dtype)


def rel_bucket(dist):
    n = jnp.maximum(dist, 0)
    max_exact = N_BUCKETS // 2
    nf = jnp.maximum(n, 1).astype(jnp.float32)
    large = max_exact + (jnp.log(nf / max_exact) / math.log(MAX_DISTANCE / max_exact)
                         * (N_BUCKETS - max_exact)).astype(jnp.int32)
    large = jnp.minimum(large, N_BUCKETS - 1)
    return jnp.where(n < max_exact, n, large)


def rel_bias_for(q_pos, k_pos, table):
    b = rel_bucket(q_pos[:, None] - k_pos[None, :])
    return jnp.moveaxis(table[b].astype(jnp.float32), -1, 0)


def lambda_init_value(layer_idx):
    return 0.8 - 0.6 * math.exp(-0.3 * layer_idx)


def diff_lambda(lq1, lk1, lq2, lk2, lam_init):
    f32 = jnp.float32
    return (jnp.exp(jnp.sum(lq1.astype(f32) * lk1.astype(f32)))
            - jnp.exp(jnp.sum(lq2.astype(f32) * lk2.astype(f32))) + lam_init)


def diff_combine(logits, lam):
    p = jax.nn.softmax(logits, axis=-1)
    return p[:, :, 0] - lam * p[:, :, 1]


def head_subln(o, g, lam_init):
    y = o * lax.rsqrt(jnp.mean(o * o, axis=-1, keepdims=True) + EPS) * g.astype(jnp.float32) * (1.0 - lam_init)
    return y.reshape(o.shape[0], o.shape[1], -1)


def diff_attn_prompt(q, k, v, lam, table):
    B, S = q.shape[0], q.shape[1]
    nb = S // Q_BLOCK
    qb = jnp.moveaxis(q.reshape(B, nb, Q_BLOCK, N_HEADS, 2, HEAD_DIM), 1, 0)
    k_pos = jnp.arange(S)
    vf = v.astype(jnp.float32)
    scale = HEAD_DIM ** -0.5

    def one_block(args):
        qblk, bi = args
        q_pos = bi * Q_BLOCK + jnp.arange(Q_BLOCK)
        logits = jnp.einsum('bqhtd,bkhtd->bhtqk', qblk, k, preferred_element_type=jnp.float32) * scale
        logits = logits + rel_bias_for(q_pos, k_pos, table)[None, :, None]
        causal = q_pos[:, None] >= k_pos[None, :]
        logits = jnp.where(causal, logits, -jnp.inf)
        a = diff_combine(logits, lam)
        return jnp.einsum('bhqk,bkhe->bqhe', a, vf)

    out = lax.map(one_block, (qb, jnp.arange(nb)))
    return jnp.moveaxis(out, 0, 1).reshape(B, S, N_HEADS, VAL_DIM)


def diff_attn_sample(q, k_new, v_new, k_past, v_past, lam, table):
    T, P = q.shape[1], k_past.shape[1]
    scale = HEAD_DIM ** -0.5
    q_pos = P + jnp.arange(T)
    lp = jnp.einsum('bqhtd,bkhtd->bhtqk', q, k_past, preferred_element_type=jnp.float32) * scale
    lp = lp + rel_bias_for(q_pos, jnp.arange(P), table)[None, :, None]
    ln = jnp.einsum('bqhtd,bkhtd->bhtqk', q, k_new, preferred_element_type=jnp.float32) * scale
    ln = ln + rel_bias_for(q_pos, q_pos, table)[None, :, None]
    causal = jnp.arange(T)[:, None] >= jnp.arange(T)[None, :]
    ln = jnp.where(causal, ln, -jnp.inf)
    a = diff_combine(jnp.concatenate([lp, ln], axis=-1), lam)
    return (jnp.einsum('bhqk,bkhe->bqhe', a[..., :P], v_past.astype(jnp.float32))
            + jnp.einsum('bhqk,bkhe->bqhe', a[..., P:], v_new.astype(jnp.float32)))


def conv_step(z, prev, w):
    z_ext = jnp.concatenate([prev.astype(z.dtype), z], axis=1)
    L = z.shape[1]
    kw = w.shape[0]
    y = z_ext[:, 0:L] * w[0]
    for j in range(1, kw):
        y = y + z_ext[:, j:j + L] * w[j]
    return y, z_ext[:, -(kw - 1):]


def mem_kv(mem, g, w):
    kv = rmsnorm(mem, g) @ w
    B, M = mem.shape[0], mem.shape[1]
    return (kv[..., :XATTN_W].reshape(B, M, N_XHEADS, XHEAD_DIM),
            kv[..., XATTN_W:].reshape(B, M, N_XHEADS, XHEAD_DIM))


def mem_attend(qx, mk, mv):
    B, T = qx.shape[0], qx.shape[1]
    q = qx.reshape(B, T, N_XHEADS, XHEAD_DIM)
    logits = jnp.einsum('bshd,bmhd->bhsm', q, mk, preferred_element_type=jnp.float32) * XHEAD_DIM ** -0.5
    p = jax.nn.softmax(logits, axis=-1)
    return jnp.einsum('bhsm,bmhd->bshd', p, mv.astype(jnp.float32)).reshape(B, T, XATTN_W)


def conv_ffn(x, g, w_up, w_conv, w_down, prev):
    h = rmsnorm(x, g) @ w_up
    hc, new_prev = conv_step(h, prev, w_conv)
    gate, up = hc[..., :D_FF], hc[..., D_FF:]
    return (jax.nn.silu(gate) * up) @ w_down, new_prev


def setup_inputs(seed: int = 0) -> dict:
    key = jax.random.key(seed)
    ks = jax.random.split(key, 32)
    f32 = jnp.float32
    n_pages = PAST_LEN // PAGE_SIZE
    n_phys = (DEC_BATCH * n_pages * 5) // 4
    nrm = lambda k, shape, s=1.0: (jax.random.normal(k, shape, f32) * s)
    page_table = jax.random.permutation(ks[0], n_phys)[:DEC_BATCH * n_pages].reshape(DEC_BATCH, n_pages).astype(jnp.int32)
    return {
        "x_prompt": nrm(ks[1], (BATCH, SEQ, D_MODEL)),
        "x_sample": nrm(ks[2], (DEC_BATCH, DEC_SEQ, D_MODEL)),
        "mem_prompt": nrm(ks[3], (BATCH, N_MEM, D_MODEL)),
        "cache_k": nrm(ks[4], (N_ATTN_LAYERS, n_phys, PAGE_SIZE, N_HEADS, VAL_DIM)),
        "cache_v": nrm(ks[5], (N_ATTN_LAYERS, n_phys, PAGE_SIZE, N_HEADS, VAL_DIM)),
        "page_table": page_table,
        "cache_mem_k": nrm(ks[6], (DEPTH, DEC_BATCH, N_MEM, N_XHEADS, XHEAD_DIM)),
        "cache_mem_v": nrm(ks[7], (DEPTH, DEC_BATCH, N_MEM, N_XHEADS, XHEAD_DIM)),
        "state_conv": nrm(ks[8], (N_CONV_LAYERS, DEC_BATCH, CONV_K - 1, MIX_W)),
        "state_ffn": nrm(ks[9], (DEPTH, DEC_BATCH, FFN_K - 1, 2 * D_FF)),
        "rel_bias": nrm(ks[10], (N_BUCKETS, N_HEADS), 0.5),
        "w_in": nrm(ks[11], (DEPTH, D_MODEL, IN_W), D_MODEL ** -0.5),
        "w_out": nrm(ks[12], (DEPTH, OUT_W, D_MODEL), OUT_W ** -0.5),
        "norm_mix": 1.0 + nrm(ks[13], (DEPTH, D_MODEL), 0.05),
        "norm_mem": 1.0 + nrm(ks[14], (DEPTH, D_MODEL), 0.05),
        "w_mem_kv": nrm(ks[15], (DEPTH, D_MODEL, 2 * XATTN_W), D_MODEL ** -0.5),
        "lambda_q1": nrm(ks[16], (N_ATTN_LAYERS, HEAD_DIM), 0.1),
        "lambda_k1": nrm(ks[17], (N_ATTN_LAYERS, HEAD_DIM), 0.1),
        "lambda_q2": nrm(ks[18], (N_ATTN_LAYERS, HEAD_DIM), 0.1),
        "lambda_k2": nrm(ks[19], (N_ATTN_LAYERS, HEAD_DIM), 0.1),
        "subln_gain": 1.0 + nrm(ks[20], (N_ATTN_LAYERS, VAL_DIM), 0.05),
        "conv_w": nrm(ks[21], (N_CONV_LAYERS, CONV_K, MIX_W), CONV_K ** -0.5),
        "norm_ffn": 1.0 + nrm(ks[22], (DEPTH, D_MODEL), 0.05),
        "w_up": nrm(ks[23], (DEPTH, D_MODEL, 2 * D_FF), D_MODEL ** -0.5),
        "ffn_conv_w": nrm(ks[24], (DEPTH, FFN_K, 2 * D_FF), FFN_K ** -0.5),
        "w_down": nrm(ks[25], (DEPTH, D_FF, D_MODEL), D_FF ** -0.5),
        "norm_final": 1.0 + nrm(ks[26], (D_MODEL,), 0.05),
    }


def reference(x_prompt, x_sample, mem_prompt, cache_k, cache_v, page_table, cache_mem_k, cache_mem_v,
              state_conv, state_ffn, rel_bias, w_in, w_out, norm_mix, norm_mem, w_mem_kv,
              lambda_q1, lambda_k1, lambda_q2, lambda_k2, subln_gain, conv_w,
              norm_ffn, w_up, ffn_conv_w, w_down, norm_final):
    xp, xs = x_prompt, x_sample
    B, S = xp.shape[0], xp.shape[1]
    DB, T = xs.shape[0], xs.shape[1]
    k_rows_p, v_rows_p, k_rows_s, v_rows_s = [], [], [], []
    mem_k_p, mem_v_p = [], []
    conv_st_p, conv_st_s, ffn_st_p, ffn_st_s = [], [], [], []

    def split_qkv(u):
        b, t = u.shape[0], u.shape[1]
        q = u[..., :MIX_W].reshape(b, t, N_HEADS, 2, HEAD_DIM)
        k = u[..., MIX_W:2 * MIX_W].reshape(b, t, N_HEADS, 2, HEAD_DIM)
        v = u[..., 2 * MIX_W:3 * MIX_W].reshape(b, t, N_HEADS, VAL_DIM)
        return q, k, v

    def short_conv(u, prev, w):
        gate_b, gate_c, h = u[..., :MIX_W], u[..., MIX_W:2 * MIX_W], u[..., 2 * MIX_W:3 * MIX_W]
        y, st = conv_step(gate_c * h, prev, w)
        return gate_b * y, st

    for i in range(DEPTH):
        li = i // 2
        mk_p, mv_p = mem_kv(mem_prompt, norm_mem[i], w_mem_kv[i])
        mem_k_p.append(mk_p)
        mem_v_p.append(mv_p)
        mk_s, mv_s = cache_mem_k[i], cache_mem_v[i]

        up = rmsnorm(xp, norm_mix[i]) @ w_in[i]
        us = rmsnorm(xs, norm_mix[i]) @ w_in[i]

        if i % 2 == 0:
            lam_init = lambda_init_value(i)
            lam = diff_lambda(lambda_q1[li], lambda_k1[li], lambda_q2[li], lambda_k2[li], lam_init)
            qp, kp, vp = split_qkv(up)
            qs, ks_, vs = split_qkv(us)
            op = diff_attn_prompt(qp, kp, vp, lam, rel_bias)
            k_past = cache_k[li, page_table].reshape(DB, -1, N_HEADS, 2, HEAD_DIM)
            v_past = cache_v[li, page_table].reshape(DB, -1, N_HEADS, VAL_DIM)
            os_ = diff_attn_sample(qs, ks_, vs, k_past, v_past, lam, rel_bias)
            mix_p = head_subln(op, subln_gain[li], lam_init)
            mix_s = head_subln(os_, subln_gain[li], lam_init)
            k_rows_p.append(kp.reshape(B, S, N_HEADS, VAL_DIM))
            v_rows_p.append(vp)
            k_rows_s.append(ks_.reshape(DB, T, N_HEADS, VAL_DIM))
            v_rows_s.append(vs)
        else:
            zeros_prev = jnp.zeros((B, CONV_K - 1, MIX_W), up.dtype)
            mix_p, st_p = short_conv(up, zeros_prev, conv_w[li])
            mix_s, st_s = short_conv(us, state_conv[li], conv_w[li])
            conv_st_p.append(st_p)
            conv_st_s.append(st_s)

        xa_p = mem_attend(up[..., 3 * MIX_W:], mk_p, mv_p)
        xa_s = mem_attend(us[..., 3 * MIX_W:], mk_s, mv_s)
        xp = xp + (jnp.concatenate([mix_p.astype(jnp.float32), xa_p], axis=-1) @ w_out[i]).astype(xp.dtype)
        xs = xs + (jnp.concatenate([mix_s.astype(jnp.float32), xa_s], axis=-1) @ w_out[i]).astype(xs.dtype)

        fp, fst_p = conv_ffn(xp, norm_ffn[i], w_up[i], ffn_conv_w[i], w_down[i],
                             jnp.zeros((B, FFN_K - 1, 2 * D_FF), xp.dtype))
        fs, fst_s = conv_ffn(xs, norm_ffn[i], w_up[i], ffn_conv_w[i], w_down[i], state_ffn[i])
        xp = xp + fp.astype(xp.dtype)
        xs = xs + fs.astype(xs.dtype)
        ffn_st_p.append(fst_p)
        ffn_st_s.append(fst_s)

    y_prompt = rmsnorm(xp, norm_final)
    y_sample = rmsnorm(xs, norm_final)
    return (y_prompt, y_sample,
            jnp.stack(k_rows_p), jnp.stack(v_rows_p), jnp.stack(k_rows_s), jnp.stack(v_rows_s),
            jnp.stack(mem_k_p), jnp.stack(mem_v_p),
            jnp.stack(conv_st_p), jnp.stack(conv_st_s), jnp.stack(ffn_st_p), jnp.stack(ffn_st_s))
```

```python
import functools
import math

import numpy as np
import jax
import jax.numpy as jnp
from jax import lax
from jax.experimental import pallas as pl
from jax.experimental.pallas import tpu as pltpu

F32 = jnp.float32
BF16 = jnp.bfloat16

HEAD_DIM = 64
VAL_DIM = 2 * HEAD_DIM
XHEAD_DIM = 128
MAX_DISTANCE = 128
EPS = 1e-6
QK_SCALE = HEAD_DIM ** -0.5
XSCALE = XHEAD_DIM ** -0.5
NEG = -1e30

LANES = 128
SUBLANES = 8
VMEM_LIMIT_MB = 56

ATTN_TILE = 512
ROW_TILE = 512
FFN_CHUNK = 256
PROJ_CHUNK = 512

NT_DIMS = (((1,), (1,)), ((), ()))


def _cparams(n_axes):
    return pltpu.CompilerParams(dimension_semantics=("arbitrary",) * n_axes,
                                vmem_limit_bytes=VMEM_LIMIT_MB * 1024 * 1024)


def _resident(block_shape, index_map):
    return pl.BlockSpec(block_shape, index_map, pipeline_mode=pl.Buffered(1))


def _rms(x, g):
    return x * lax.rsqrt(jnp.mean(x * x, axis=-1, keepdims=True) + EPS) * g


def _lambda_init(layer_idx):
    return 0.8 - 0.6 * math.exp(-0.3 * layer_idx)


def _lambda(lamp):
    a = jnp.sum(lamp[0:1] * lamp[1:2], axis=-1, keepdims=True)
    b = jnp.sum(lamp[2:3] * lamp[3:4], axis=-1, keepdims=True)
    return jnp.exp(a) - jnp.exp(b)


def _bucket_np(n, n_buckets):
    max_exact = n_buckets // 2
    nf = np.maximum(n, 1).astype(np.float32)
    large = max_exact + (np.log(nf / np.float32(max_exact)) / np.float32(math.log(MAX_DISTANCE / max_exact))
                         * np.float32(n_buckets - max_exact)).astype(np.int32)
    large = np.minimum(large, n_buckets - 1)
    return np.where(n < max_exact, n, large).astype(np.int32)


def _causal_conv3(z, prev, w):
    row = lax.broadcasted_iota(jnp.int32, z.shape, 0)
    p1 = prev[SUBLANES - 1:SUBLANES, :]
    p2 = prev[SUBLANES - 2:SUBLANES - 1, :]
    z1 = jnp.where(row == 0, p1, pltpu.roll(z, 1, 0))
    z2 = jnp.where(row == 0, p2, jnp.where(row == 1, p1, pltpu.roll(z, 2, 0)))
    return z2 * w[0:1, :] + z1 * w[1:2, :] + z * w[2:3, :]


def _silu(g):
    return g * (1.0 / (1.0 + jnp.exp(-g)))


def _memkv_body(mem_ref, g_ref, w_ref, kf_ref, vf_ref, kb_ref, vb_ref):
    xw = kf_ref.shape[-1]
    xn = _rms(mem_ref[...], g_ref[...]).astype(BF16)
    k = jnp.dot(xn, w_ref[:, :xw], preferred_element_type=F32)
    v = jnp.dot(xn, w_ref[:, xw:], preferred_element_type=F32)
    kf_ref[...] = k
    vf_ref[...] = v
    kb_ref[...] = k.astype(BF16)
    vb_ref[...] = v.astype(BF16)


def _memkv(mem2d, norm_mem, w_mem_bf):
    depth, d, two_xw = w_mem_bf.shape
    xw = two_xw // 2
    rows = mem2d.shape[0]
    out = lambda dt: jax.ShapeDtypeStruct((depth, rows, xw), dt)
    ospec = pl.BlockSpec((None, rows, xw), lambda l: (l, 0, 0))
    return pl.pallas_call(
        _memkv_body,
        grid=(depth,),
        in_specs=[pl.BlockSpec((rows, d), lambda l: (0, 0)),
                  pl.BlockSpec((None, 1, d), lambda l: (l, 0, 0)),
                  pl.BlockSpec((None, d, two_xw), lambda l: (l, 0, 0))],
        out_specs=[ospec, ospec, ospec, ospec],
        out_shape=[out(F32), out(F32), out(BF16), out(BF16)],
        compiler_params=_cparams(1),
        name="mem_kv",
    )(mem2d, norm_mem.reshape(depth, 1, d), w_mem_bf)


def _proj_attn_body(x_ref, g_ref, w_ref, q_ref, kf_ref, vf_ref, kb_ref, vb_ref, qx_ref):
    mw = q_ref.shape[-1]
    xw = qx_ref.shape[-1]
    xn = _rms(x_ref[...], g_ref[...]).astype(BF16)

    def mm(lo, n):
        return jnp.dot(xn, w_ref[:, lo:lo + n], preferred_element_type=F32)

    n = PROJ_CHUNK
    for c in range(0, mw, n):
        q_ref[:, c:c + n] = (mm(c, n) * QK_SCALE).astype(BF16)
        k = mm(mw + c, n)
        kf_ref[:, c:c + n] = k
        kb_ref[:, c:c + n] = k.astype(BF16)
        v = mm(2 * mw + c, n)
        vf_ref[:, c:c + n] = v
        vb_ref[:, c:c + n] = v.astype(BF16)
    qx_ref[...] = mm(3 * mw, xw).astype(BF16)


def _proj_attn(x2d, norm, w_in_bf, layer, tm, mw, xw):
    m, d = x2d.shape
    in_w = w_in_bf.shape[-1]
    row = lambda w: pl.BlockSpec((tm, w), lambda i: (i, 0))
    return pl.pallas_call(
        _proj_attn_body,
        grid=(m // tm,),
        in_specs=[row(d),
                  pl.BlockSpec((None, 1, d), lambda i: (layer, 0, 0)),
                  _resident((None, d, in_w), lambda i: (layer, 0, 0))],
        out_specs=[row(mw), row(mw), row(mw), row(mw), row(mw), row(xw)],
        out_shape=[jax.ShapeDtypeStruct((m, mw), BF16),
                   jax.ShapeDtypeStruct((m, mw), F32),
                   jax.ShapeDtypeStruct((m, mw), F32),
                   jax.ShapeDtypeStruct((m, mw), BF16),
                   jax.ShapeDtypeStruct((m, mw), BF16),
                   jax.ShapeDtypeStruct((m, xw), BF16)],
        compiler_params=_cparams(1),
        name="proj_attn",
    )(x2d, norm, w_in_bf)


def _proj_conv_body(x_ref, g_ref, w_ref, cw_ref, mix_ref, qx_ref, st_ref, *, tiles_per_seq):
    mw = mix_ref.shape[-1]
    xw = qx_ref.shape[-1]
    tm = x_ref.shape[0]
    first = (pl.program_id(0) % tiles_per_seq) == 0
    xn = _rms(x_ref[...], g_ref[...]).astype(BF16)

    def mm(lo, n):
        return jnp.dot(xn, w_ref[:, lo:lo + n], preferred_element_type=F32)

    n = PROJ_CHUNK
    for c in range(0, mw, n):
        z = mm(mw + c, n) * mm(2 * mw + c, n)
        prev = jnp.where(first, 0.0, st_ref[:, c:c + n])
        y = _causal_conv3(z, prev, cw_ref[:, c:c + n])
        st_ref[:, c:c + n] = z[tm - SUBLANES:, :]
        mix_ref[:, c:c + n] = (mm(c, n) * y).astype(BF16)
    qx_ref[...] = mm(3 * mw, xw).astype(BF16)


def _proj_conv(x2d, norm, w_in_bf, conv_w, layer, conv_layer, tm, seq, mw, xw):
    m, d = x2d.shape
    in_w = w_in_bf.shape[-1]
    tps = seq // tm
    row = lambda w: pl.BlockSpec((tm, w), lambda i: (i, 0))
    return pl.pallas_call(
        functools.partial(_proj_conv_body, tiles_per_seq=tps),
        grid=(m // tm,),
        in_specs=[row(d),
                  pl.BlockSpec((None, 1, d), lambda i: (layer, 0, 0)),
                  _resident((None, d, in_w), lambda i: (layer, 0, 0)),
                  pl.BlockSpec((None, conv_w.shape[1], mw), lambda i: (conv_layer, 0, 0))],
        out_specs=[row(mw), row(xw),
                   pl.BlockSpec((None, SUBLANES, mw), lambda i: (i // tps, 0, 0))],
        out_shape=[jax.ShapeDtypeStruct((m, mw), BF16),
                   jax.ShapeDtypeStruct((m, xw), BF16),
                   jax.ShapeDtypeStruct((m // seq, SUBLANES, mw), F32)],
        compiler_params=_cparams(1),
        name="proj_conv",
    )(x2d, norm, w_in_bf, conv_w)


def _attn_body(tab_ref, q_ref, k_ref, vt_ref, bk_ref, lamp_ref, g_ref, o_ref,
               acc1, acc2, st, bdiag, bprev, *, lam_init):
    t = q_ref.shape[0]
    n_buckets = tab_ref.shape[0]
    h = pl.program_id(1)
    qi = pl.program_id(2)
    nb = t // LANES

    @pl.when(qi == 0)
    def _build_bias():
        far = tab_ref[n_buckets - 1, h]

        def tile(bk):
            out = jnp.zeros(bk.shape, F32)
            for b in range(n_buckets - 1):
                out = jnp.where(bk == b, tab_ref[b, h] - far, out)
            return jnp.where(bk < 0, NEG, out)

        d0 = tile(bk_ref[0])
        d1 = tile(bk_ref[1])
        bdiag[...] = jnp.zeros((t, t), F32)
        bprev[...] = jnp.zeros((t, t), F32)
        for r in range(nb):
            for c in range(nb):
                sub = (slice(r * LANES, (r + 1) * LANES), slice(c * LANES, (c + 1) * LANES))
                if c < r:
                    bdiag[sub] = jnp.full((LANES, LANES), NEG, F32)
                elif c == r:
                    bdiag[sub] = d0
                elif c == r + 1:
                    bdiag[sub] = d1
        bprev[(nb - 1) * LANES:, 0:LANES] = d1

    q = q_ref[...]
    lane = lax.broadcasted_iota(jnp.int32, q.shape, 1)
    zero = jnp.zeros_like(q)
    q1 = jnp.where(lane < HEAD_DIM, q, zero)
    q2 = jnp.where(lane >= HEAD_DIM, q, zero)

    acc1[...] = jnp.zeros(acc1.shape, F32)
    acc2[...] = jnp.zeros(acc2.shape, F32)
    st[0:1, :] = jnp.full((1, t), NEG, F32)
    st[1:2, :] = jnp.zeros((1, t), F32)
    st[2:3, :] = jnp.full((1, t), NEG, F32)
    st[3:4, :] = jnp.zeros((1, t), F32)

    def block(j, bias_ref):
        kb = k_ref[pl.ds(pl.multiple_of(j * t, t), t), :]
        vt = vt_ref[j]
        for qz, acc, r in ((q1, acc1, 0), (q2, acc2, 2)):
            s = lax.dot_general(kb, qz, NT_DIMS, preferred_element_type=F32)
            if bias_ref is not None:
                s = s + bias_ref[...]
            m_old = st[r:r + 1, :]
            m_new = jnp.maximum(m_old, jnp.max(s, axis=0, keepdims=True))
            alpha = jnp.exp(m_old - m_new)
            p = jnp.exp(s - m_new)
            st[r + 1:r + 2, :] = alpha * st[r + 1:r + 2, :] + jnp.sum(p, axis=0, keepdims=True)
            st[r:r + 1, :] = m_new
            acc[...] = alpha * acc[...] + jnp.dot(vt, p.astype(BF16), preferred_element_type=F32)

    def far_block(j, carry):
        block(j, None)
        return carry

    lax.fori_loop(0, jnp.maximum(qi - 1, 0), far_block, 0)

    @pl.when(qi > 0)
    def _prev():
        block(qi - 1, bprev)

    block(qi, bdiag)

    lam = _lambda(lamp_ref[...]) + lam_init
    o_t = acc1[...] * (1.0 / st[1:2, :]) - lam * (acc2[...] * (1.0 / st[3:4, :]))
    o = o_t.T
    y = o * lax.rsqrt(jnp.mean(o * o, axis=-1, keepdims=True) + EPS) * g_ref[...] * (1.0 - lam_init)
    o_ref[...] = y.astype(o_ref.dtype)


def _attn_bucket_tiles(n_buckets):
    kk = np.arange(LANES)[:, None]
    qq = np.arange(LANES)[None, :]
    d0 = qq - kk
    t0 = np.where(d0 >= 0, _bucket_np(np.maximum(d0, 0), n_buckets), -1)
    t1 = _bucket_np(LANES + qq - kk, n_buckets)
    return np.stack([t0, t1]).astype(np.int32)


def _attention(q_bf, k_bf, vt_bf, rel_bias, lamp, gain, batch, seq, n_heads, layer):
    t = ATTN_TILE
    nq = seq // t
    assert seq % t == 0 and t >= 2 * LANES and MAX_DISTANCE <= LANES
    m, mw = q_bf.shape
    bk = jnp.asarray(_attn_bucket_tiles(rel_bias.shape[0]))
    return pl.pallas_call(
        functools.partial(_attn_body, lam_init=_lambda_init(layer)),
        grid=(batch, n_heads, nq),
        in_specs=[pl.BlockSpec(memory_space=pltpu.SMEM),
                  pl.BlockSpec((t, VAL_DIM), lambda b, h, i: (b * nq + i, h)),
                  pl.BlockSpec((seq, VAL_DIM), lambda b, h, i: (b, h)),
                  pl.BlockSpec((None, None, nq, VAL_DIM, t), lambda b, h, i: (b, h, 0, 0, 0)),
                  pl.BlockSpec((2, LANES, LANES), lambda b, h, i: (0, 0, 0)),
                  pl.BlockSpec((4, HEAD_DIM), lambda b, h, i: (0, 0)),
                  pl.BlockSpec((1, VAL_DIM), lambda b, h, i: (0, 0))],
        out_specs=pl.BlockSpec((t, VAL_DIM), lambda b, h, i: (b * nq + i, h)),
        out_shape=jax.ShapeDtypeStruct((m, mw), BF16),
        scratch_shapes=[pltpu.VMEM((VAL_DIM, t), F32), pltpu.VMEM((VAL_DIM, t), F32),
                        pltpu.VMEM((SUBLANES, t), F32),
                        pltpu.VMEM((t, t), F32), pltpu.VMEM((t, t), F32)],
        compiler_params=_cparams(3),
        name="diff_attn",
    )(rel_bias, q_bf, k_bf, vt_bf, bk, lamp, gain)


def _out_body(mix_ref, qx_ref, mk_ref, mv_ref, w_ref, x_ref, o_ref):
    mw = mix_ref.shape[-1]
    xw = qx_ref.shape[-1]
    acc = jnp.dot(mix_ref[...], w_ref[0:mw, :], preferred_element_type=F32)
    for c in range(0, xw, XHEAD_DIM):
        hs = slice(c, c + XHEAD_DIM)
        s = lax.dot_general(qx_ref[:, hs], mk_ref[:, hs], NT_DIMS, preferred_element_type=F32) * XSCALE
        p = jnp.exp(s - jnp.max(s, axis=-1, keepdims=True))
        inv = 1.0 / jnp.sum(p, axis=-1, keepdims=True)
        xa = jnp.dot(p.astype(BF16), mv_ref[:, hs], preferred_element_type=F32) * inv
        acc = acc + jnp.dot(xa.astype(BF16), w_ref[mw + c:mw + c + XHEAD_DIM, :],
                            preferred_element_type=F32)
    o_ref[...] = x_ref[...] + acc


def _out_proj(mix_bf, qx_bf, mk_bf, mv_bf, w_out_bf, x2d, layer, tm, seq):
    m, d = x2d.shape
    mw = mix_bf.shape[-1]
    xw = qx_bf.shape[-1]
    n_mem = mk_bf.shape[1] // (m // seq)
    tps = seq // tm
    row = lambda w: pl.BlockSpec((tm, w), lambda i: (i, 0))
    mem = pl.BlockSpec((None, n_mem, xw), lambda i: (layer, i // tps, 0))
    return pl.pallas_call(
        _out_body,
        grid=(m // tm,),
        in_specs=[row(mw), row(xw), mem, mem,
                  _resident((None, mw + xw, d), lambda i: (layer, 0, 0)),
                  row(d)],
        out_specs=row(d),
        out_shape=jax.ShapeDtypeStruct((m, d), F32),
        compiler_params=_cparams(1),
        name="out_proj",
    )(mix_bf, qx_bf, mk_bf, mv_bf, w_out_bf, x2d)


def _ffn_body(x_ref, g_ref, wu_ref, cw_ref, wd_ref, gf_ref, o_ref, st_ref, act_ref,
              *, tiles_per_seq, final_norm):
    tm = x_ref.shape[0]
    dff = wd_ref.shape[0]
    first = (pl.program_id(0) % tiles_per_seq) == 0
    x = x_ref[...]
    xn = _rms(x, g_ref[...]).astype(BF16)

    def conv_cols(lo, n):
        h = jnp.dot(xn, wu_ref[:, lo:lo + n], preferred_element_type=F32)
        prev = jnp.where(first, 0.0, st_ref[:, lo:lo + n])
        st_ref[:, lo:lo + n] = h[tm - SUBLANES:, :]
        return _causal_conv3(h, prev, cw_ref[:, lo:lo + n])

    n = FFN_CHUNK
    for c in range(0, dff, n):
        act_ref[:, c:c + n] = (_silu(conv_cols(c, n)) * conv_cols(dff + c, n)).astype(BF16)
    y = x + jnp.dot(act_ref[...], wd_ref[...], preferred_element_type=F32)
    if final_norm:
        y = _rms(y, gf_ref[...])
    o_ref[...] = y


def _ffn(x2d, norm, w_up_bf, ffn_conv_w, w_down_bf, norm_final, layer, tm, seq, final_norm):
    m, d = x2d.shape
    dff = w_down_bf.shape[1]
    assert dff % FFN_CHUNK == 0
    tps = seq // tm
    row = pl.BlockSpec((tm, d), lambda i: (i, 0))
    return pl.pallas_call(
        functools.partial(_ffn_body, tiles_per_seq=tps, final_norm=final_norm),
        grid=(m // tm,),
        in_specs=[row,
                  pl.BlockSpec((None, 1, d), lambda i: (layer, 0, 0)),
                  _resident((None, d, 2 * dff), lambda i: (layer, 0, 0)),
                  pl.BlockSpec((None, ffn_conv_w.shape[1], 2 * dff), lambda i: (layer, 0, 0)),
                  _resident((None, dff, d), lambda i: (layer, 0, 0)),
                  pl.BlockSpec((1, d), lambda i: (0, 0))],
        out_specs=[row, pl.BlockSpec((None, SUBLANES, 2 * dff), lambda i: (i // tps, 0, 0))],
        out_shape=[jax.ShapeDtypeStruct((m, d), F32),
                   jax.ShapeDtypeStruct((m // seq, SUBLANES, 2 * dff), F32)],
        scratch_shapes=[pltpu.VMEM((tm, dff), BF16)],
        compiler_params=_cparams(1),
        name="conv_ffn",
    )(x2d, norm, w_up_bf, ffn_conv_w, w_down_bf, norm_final)


def _proj_plain_body(x_ref, g_ref, w_ref, u_ref):
    xn = _rms(x_ref[...], g_ref[...]).astype(BF16)
    u_ref[...] = jnp.dot(xn, w_ref[...], preferred_element_type=F32)


def _proj_plain(x2d, norm, w_in_bf, layer):
    m, d = x2d.shape
    in_w = w_in_bf.shape[-1]
    n = PROJ_CHUNK
    return pl.pallas_call(
        _proj_plain_body,
        grid=(in_w // n,),
        in_specs=[pl.BlockSpec((m, d), lambda c: (0, 0)),
                  pl.BlockSpec((None, 1, d), lambda c: (layer, 0, 0)),
                  pl.BlockSpec((None, d, n), lambda c: (layer, 0, c))],
        out_specs=pl.BlockSpec((m, n), lambda c: (0, c)),
        out_shape=jax.ShapeDtypeStruct((m, in_w), F32),
        compiler_params=_cparams(1),
        name="proj_sample",
    )(x2d, norm, w_in_bf)


def _paged_attn_body(pt_ref, q_ref, kn_ref, vn_ref, bks_ref, tab_ref, lamp_ref, *rest,
                     n_pages, lam_init):
    k_refs = rest[:n_pages]
    v_refs = rest[n_pages:2 * n_pages]
    o_ref, s_ref, bias_ref = rest[2 * n_pages:]
    page, width = k_refs[0].shape
    past = n_pages * page
    n_buckets = tab_ref.shape[0]
    n_rows = width // HEAD_DIM

    @pl.when(pl.program_id(0) == 0)
    def _build_bias():
        bk = bks_ref[...]
        out = jnp.zeros(bk.shape, F32)
        for b in range(n_buckets):
            out = jnp.where(bk == b, tab_ref[b:b + 1, :], out)
        bias_ref[...] = jnp.where(bk < 0, NEG, out)

    q = q_ref[...] * QK_SCALE
    lane = lax.broadcasted_iota(jnp.int32, (LANES, width), 1)
    row = lax.broadcasted_iota(jnp.int32, (LANES, width), 0)
    qrows = jnp.where(lane // HEAD_DIM == row, q, 0.0)

    for r in range(n_pages):
        s_ref[r * page:(r + 1) * page, :] = lax.dot_general(
            k_refs[r][...], qrows, NT_DIMS, preferred_element_type=F32)
    kn = jnp.where(row == 0, kn_ref[...], 0.0)
    s_ref[past:, :] = lax.dot_general(kn, qrows, NT_DIMS, preferred_element_type=F32)

    s = s_ref[...] + bias_ref[...]
    m = jnp.max(s, axis=0, keepdims=True)
    p = jnp.exp(s - m)
    pn = p * (1.0 / jnp.sum(p, axis=0, keepdims=True))
    pt = pn.T
    pv = jnp.dot(pt[0:n_rows, past:], jnp.where(row == 0, vn_ref[...], 0.0),
                 preferred_element_type=F32)
    for r in range(n_pages):
        pv = pv + jnp.dot(pt[0:n_rows, r * page:(r + 1) * page], v_refs[r][...],
                          preferred_element_type=F32)
    lam = _lambda(lamp_ref[...]) + lam_init
    prow = lax.broadcasted_iota(jnp.int32, pv.shape, 0)
    plane = lax.broadcasted_iota(jnp.int32, pv.shape, 1)
    coef = jnp.where(prow % 2 == 0, 1.0, -lam)
    sel = jnp.where(plane // VAL_DIM == prow // 2, coef * pv, 0.0)
    o_ref[...] = jnp.sum(sel, axis=0, keepdims=True)


def _paged_attention(q, k_new, v_new, cache_k, cache_v, page_table, rel_bias, lamp, layer, attn_layer):
    db, width = q.shape
    n_layers, n_phys, page = cache_k.shape[:3]
    n_pages = page_table.shape[1]
    past = n_pages * page
    n_buckets, n_heads = rel_bias.shape
    assert page == LANES and width // HEAD_DIM <= LANES
    ck = cache_k.reshape(n_layers, n_phys, page, width)
    cv = cache_v.reshape(n_layers, n_phys, page, width)
    dist = past - np.arange(past + LANES)
    bks = np.where(dist >= 0, _bucket_np(np.maximum(dist, 0), n_buckets), -1)
    bks = jnp.asarray(np.broadcast_to(bks[:, None], (past + LANES, LANES)).astype(np.int32))
    tab = jnp.zeros((n_buckets, LANES), F32).at[:, :2 * n_heads].set(jnp.repeat(rel_bias, 2, axis=1))
    vec = lambda a: a.reshape(db, 1, width)
    vspec = pl.BlockSpec((None, 1, width), lambda b, pt: (b, 0, 0))

    def page_spec(r):
        return pl.BlockSpec((None, None, page, width), lambda b, pt: (attn_layer, pt[b, r], 0, 0))

    const = lambda shape: pl.BlockSpec(shape, lambda b, pt: (0,) * len(shape))
    grid_spec = pltpu.PrefetchScalarGridSpec(
        num_scalar_prefetch=1,
        grid=(db,),
        in_specs=[vspec, vspec, vspec,
                  const((past + LANES, LANES)), const((n_buckets, LANES)), const((4, HEAD_DIM))]
                 + [page_spec(r) for r in range(n_pages)] * 2,
        out_specs=vspec,
        scratch_shapes=[pltpu.VMEM((past + LANES, LANES), F32), pltpu.VMEM((past + LANES, LANES), F32)],
    )
    out = pl.pallas_call(
        functools.partial(_paged_attn_body, n_pages=n_pages, lam_init=_lambda_init(layer)),
        grid_spec=grid_spec,
        out_shape=jax.ShapeDtypeStruct((db, 1, width), F32),
        compiler_params=_cparams(1),
        name="paged_attn",
    )(page_table, vec(q), vec(k_new), vec(v_new), bks, tab, lamp,
      *([ck] * n_pages), *([cv] * n_pages))
    return out.reshape(db, width)


def _mem_sample_body(qx_ref, mk_ref, mv_ref, o_ref):
    g, xw = qx_ref.shape
    lane = lax.broadcasted_iota(jnp.int32, (SUBLANES, xw), 1)
    row = lax.broadcasted_iota(jnp.int32, (SUBLANES, xw), 0)
    own = lane // XHEAD_DIM == row
    for b in range(g):
        qrows = jnp.where(own, qx_ref[b:b + 1, :], 0.0)
        s = lax.dot_general(qrows, mk_ref[b], NT_DIMS, preferred_element_type=F32) * XSCALE
        p = jnp.exp(s - jnp.max(s, axis=-1, keepdims=True))
        pn = p * (1.0 / jnp.sum(p, axis=-1, keepdims=True))
        full = jnp.dot(pn, mv_ref[b], preferred_element_type=F32)
        o_ref[b:b + 1, :] = jnp.sum(jnp.where(own, full, 0.0), axis=0, keepdims=True)


def _mem_sample(u, cache_mem_k, cache_mem_v, layer, mw, xw):
    db = u.shape[0]
    n_mem = cache_mem_k.shape[2]
    g = SUBLANES
    mk = cache_mem_k.reshape(cache_mem_k.shape[0], db, n_mem, xw)
    mv = cache_mem_v.reshape(cache_mem_v.shape[0], db, n_mem, xw)
    mem = pl.BlockSpec((None, g, n_mem, xw), lambda i: (layer, i, 0, 0))
    return pl.pallas_call(
        _mem_sample_body,
        grid=(db // g,),
        in_specs=[pl.BlockSpec((g, xw), lambda i: (i, 3 * mw // xw)), mem, mem],
        out_specs=pl.BlockSpec((g, xw), lambda i: (i, 0)),
        out_shape=jax.ShapeDtypeStruct((db, xw), F32),
        compiler_params=_cparams(1),
        name="mem_attn_sample",
    )(u, mk, mv)


def _out_sample_attn_body(o_ref, g_ref, xa_ref, w_ref, x_ref, y_ref, *, lam_init):
    mw = o_ref.shape[-1]
    acc = jnp.dot(xa_ref[...].astype(BF16), w_ref[mw:, :], preferred_element_type=F32)
    for c in range(0, mw, VAL_DIM):
        o = o_ref[:, c:c + VAL_DIM]
        mix = o * lax.rsqrt(jnp.mean(o * o, axis=-1, keepdims=True) + EPS) * g_ref[...] * (1.0 - lam_init)
        acc = acc + jnp.dot(mix.astype(BF16), w_ref[c:c + VAL_DIM, :], preferred_element_type=F32)
    y_ref[...] = x_ref[...] + acc


def _out_sample_conv_body(u_ref, st_ref, cw_ref, xa_ref, w_ref, x_ref, y_ref, z_ref):
    mw = z_ref.shape[-1]
    z = u_ref[:, mw:2 * mw] * u_ref[:, 2 * mw:3 * mw]
    conv = st_ref[:, 0:mw] * cw_ref[0:1, :] + st_ref[:, mw:] * cw_ref[1:2, :] + z * cw_ref[2:3, :]
    mix = u_ref[:, 0:mw] * conv
    z_ref[...] = z
    acc = jnp.dot(mix.astype(BF16), w_ref[0:mw, :], preferred_element_type=F32)
    acc = acc + jnp.dot(xa_ref[...].astype(BF16), w_ref[mw:, :], preferred_element_type=F32)
    y_ref[...] = x_ref[...] + acc


def _full(a):
    return pl.BlockSpec(a.shape, lambda i: (0,) * a.ndim)


def _out_sample_attn(o, gain, xa, w_out_bf, x2d, layer):
    w_spec = pl.BlockSpec((None,) + w_out_bf.shape[1:], lambda i: (layer, 0, 0))
    return pl.pallas_call(
        functools.partial(_out_sample_attn_body, lam_init=_lambda_init(layer)),
        grid=(1,),
        in_specs=[_full(o), _full(gain), _full(xa), w_spec, _full(x2d)],
        out_specs=_full(x2d),
        out_shape=jax.ShapeDtypeStruct(x2d.shape, F32),
        compiler_params=_cparams(1),
        name="out_sample_attn",
    )(o, gain, xa, w_out_bf, x2d)


def _out_sample_conv(u, state2d, conv_w, xa, w_out_bf, x2d, layer, conv_layer, mw):
    db = u.shape[0]
    w_spec = pl.BlockSpec((None,) + w_out_bf.shape[1:], lambda i: (layer, 0, 0))
    cw_spec = pl.BlockSpec((None,) + conv_w.shape[1:], lambda i: (conv_layer, 0, 0))
    z_shape = jax.ShapeDtypeStruct((db, mw), F32)
    return pl.pallas_call(
        _out_sample_conv_body,
        grid=(1,),
        in_specs=[_full(u), _full(state2d), cw_spec, _full(xa), w_spec, _full(x2d)],
        out_specs=[_full(x2d), _full(z_shape)],
        out_shape=[jax.ShapeDtypeStruct(x2d.shape, F32), z_shape],
        compiler_params=_cparams(1),
        name="out_sample_conv",
    )(u, state2d, conv_w, xa, w_out_bf, x2d)


def _ffn_sample_body(x_ref, g_ref, wg_ref, wu_ref, cg_ref, cu_ref, s0g_ref, s0u_ref, s1g_ref, s1u_ref,
                     wd_ref, gf_ref, y_ref, hg_ref, hu_ref, xn_ref, *, final_norm):
    c = pl.program_id(0)

    @pl.when(c == 0)
    def _init():
        xn_ref[...] = _rms(x_ref[...], g_ref[...]).astype(BF16)
        y_ref[...] = x_ref[...]

    def step(w_ref, cw_ref, s0_ref, s1_ref, h_ref):
        h = jnp.dot(xn_ref[...], w_ref[...], preferred_element_type=F32)
        h_ref[...] = h
        return s0_ref[...] * cw_ref[0:1, :] + s1_ref[...] * cw_ref[1:2, :] + h * cw_ref[2:3, :]

    gate = step(wg_ref, cg_ref, s0g_ref, s1g_ref, hg_ref)
    up = step(wu_ref, cu_ref, s0u_ref, s1u_ref, hu_ref)
    act = (_silu(gate) * up).astype(BF16)
    y_ref[...] += jnp.dot(act, wd_ref[...], preferred_element_type=F32)

    if final_norm:
        @pl.when(c == pl.num_programs(0) - 1)
        def _finish():
            y_ref[...] = _rms(y_ref[...], gf_ref[...])


def _ffn_sample(x2d, norm, w_up_bf, ffn_conv_w, w_down_bf, state2d, norm_final, layer, final_norm):
    m, d = x2d.shape
    dff = w_down_bf.shape[1]
    n = FFN_CHUNK
    nch = dff // n
    kw = ffn_conv_w.shape[1]
    xs = pl.BlockSpec((m, d), lambda c: (0, 0))
    col = lambda off: pl.BlockSpec((m, n), lambda c: (0, off + c))
    return pl.pallas_call(
        functools.partial(_ffn_sample_body, final_norm=final_norm),
        grid=(nch,),
        in_specs=[xs,
                  pl.BlockSpec((None, 1, d), lambda c: (layer, 0, 0)),
                  pl.BlockSpec((None, d, n), lambda c: (layer, 0, c)),
                  pl.BlockSpec((None, d, n), lambda c: (layer, 0, nch + c)),
                  pl.BlockSpec((None, kw, n), lambda c: (layer, 0, c)),
                  pl.BlockSpec((None, kw, n), lambda c: (layer, 0, nch + c)),
                  col(0), col(nch), col(2 * nch), col(3 * nch),
                  pl.BlockSpec((None, n, d), lambda c: (layer, c, 0)),
                  pl.BlockSpec((1, d), lambda c: (0, 0))],
        out_specs=[xs, col(0), col(0)],
        out_shape=[jax.ShapeDtypeStruct((m, d), F32),
                   jax.ShapeDtypeStruct((m, dff), F32),
                   jax.ShapeDtypeStruct((m, dff), F32)],
        scratch_shapes=[pltpu.VMEM((m, d), BF16)],
        compiler_params=_cparams(1),
        name="conv_ffn_sample",
    )(x2d, norm, w_up_bf, w_up_bf, ffn_conv_w, ffn_conv_w, state2d, state2d, state2d, state2d,
      w_down_bf, norm_final)


def kernel(x_prompt, x_sample, mem_prompt, cache_k, cache_v, page_table, cache_mem_k, cache_mem_v,
           state_conv, state_ffn, rel_bias, w_in, w_out, norm_mix, norm_mem, w_mem_kv,
           lambda_q1, lambda_k1, lambda_q2, lambda_k2, subln_gain, conv_w,
           norm_ffn, w_up, ffn_conv_w, w_down, norm_final):
    batch, seq, d = x_prompt.shape
    db = x_sample.shape[0]
    depth = w_in.shape[0]
    n_mem = mem_prompt.shape[1]
    n_heads = cache_k.shape[3]
    mw = n_heads * VAL_DIM
    xw = w_mem_kv.shape[-1] // 2
    n_xheads = xw // XHEAD_DIM
    dff = w_down.shape[1]
    tm = min(ROW_TILE, seq)
    assert x_sample.shape[1] == 1 and seq % tm == 0 and w_in.shape[-1] == 3 * mw + xw

    w_in_bf = w_in.astype(BF16)
    w_out_bf = w_out.astype(BF16)
    w_up_bf = w_up.astype(BF16)
    w_down_bf = w_down.astype(BF16)
    w_mem_bf = w_mem_kv.astype(BF16)
    norm_mix3 = norm_mix.reshape(depth, 1, d)
    norm_ffn3 = norm_ffn.reshape(depth, 1, d)
    norm_final2 = norm_final.reshape(1, d)
    nt = seq // ATTN_TILE

    xp = x_prompt.reshape(batch * seq, d)
    xs = x_sample.reshape(db, d)

    mk_f, mv_f, mk_bf, mv_bf = _memkv(mem_prompt.reshape(batch * n_mem, d), norm_mem, w_mem_bf)

    k_rows_p, v_rows_p, k_rows_s, v_rows_s = [], [], [], []
    conv_st_p, conv_st_s, ffn_st_p, ffn_st_s = [], [], [], []

    for i in range(depth):
        li = i // 2
        last = i == depth - 1
        if i % 2 == 0:
            lamp = jnp.stack([lambda_q1[li], lambda_k1[li], lambda_q2[li], lambda_k2[li]])
            gain = subln_gain[li].reshape(1, VAL_DIM)
            q_bf, k_f, v_f, k_bf, v_bf, qx_bf = _proj_attn(xp, norm_mix3, w_in_bf, i, tm, mw, xw)
            vt_bf = v_bf.reshape(batch, nt, ATTN_TILE, n_heads, VAL_DIM).transpose(0, 3, 1, 4, 2)
            mix_bf = _attention(q_bf, k_bf, vt_bf, rel_bias, lamp, gain, batch, seq, n_heads, i)
            k_rows_p.append(k_f.reshape(batch, seq, n_heads, VAL_DIM))
            v_rows_p.append(v_f.reshape(batch, seq, n_heads, VAL_DIM))
            us = _proj_plain(xs, norm_mix3, w_in_bf, i)
            qs, ks, vs = us[:, :mw], us[:, mw:2 * mw], us[:, 2 * mw:3 * mw]
            o_s = _paged_attention(qs, ks, vs, cache_k, cache_v, page_table, rel_bias, lamp, i, li)
            xa_s = _mem_sample(us, cache_mem_k, cache_mem_v, i, mw, xw)
            xs = _out_sample_attn(o_s, gain, xa_s, w_out_bf, xs, i)
            k_rows_s.append(ks.reshape(db, 1, n_heads, VAL_DIM))
            v_rows_s.append(vs.reshape(db, 1, n_heads, VAL_DIM))
        else:
            mix_bf, qx_bf, cst = _proj_conv(xp, norm_mix3, w_in_bf, conv_w, i, li, tm, seq, mw, xw)
            conv_st_p.append(cst[:, SUBLANES - (conv_w.shape[1] - 1):, :])
            us = _proj_plain(xs, norm_mix3, w_in_bf, i)
            xa_s = _mem_sample(us, cache_mem_k, cache_mem_v, i, mw, xw)
            st = state_conv[li]
            xs, z_s = _out_sample_conv(us, st.reshape(db, -1), conv_w, xa_s, w_out_bf, xs, i, li, mw)
            conv_st_s.append(jnp.stack([st[:, 1, :], z_s], axis=1))

        xp = _out_proj(mix_bf, qx_bf, mk_bf, mv_bf, w_out_bf, xp, i, tm, seq)
        xp, fst = _ffn(xp, norm_ffn3, w_up_bf, ffn_conv_w, w_down_bf, norm_final2, i, tm, seq, last)
        ffn_st_p.append(fst[:, SUBLANES - (ffn_conv_w.shape[1] - 1):, :])

        fs = state_ffn[i]
        xs, hg, hu = _ffn_sample(xs, norm_ffn3, w_up_bf, ffn_conv_w, w_down_bf, fs.reshape(db, -1),
                                 norm_final2, i, last)
        ffn_st_s.append(jnp.stack([fs[:, 1, :], jnp.concatenate([hg, hu], axis=-1)], axis=1))

    mem_shape = (depth, batch, n_mem, n_xheads, XHEAD_DIM)
    return (xp.reshape(batch, seq, d), xs.reshape(db, 1, d),
            jnp.stack(k_rows_p), jnp.stack(v_rows_p), jnp.stack(k_rows_s), jnp.stack(v_rows_s),
            mk_f.reshape(mem_shape), mv_f.reshape(mem_shape),
            jnp.stack(conv_st_p), jnp.stack(conv_st_s), jnp.stack(ffn_st_p), jnp.stack(ffn_st_s))
```

```python
import functools
import math

import numpy as np
import jax
import jax.numpy as jnp
from jax import lax
from jax.experimental import pallas as pl
from jax.experimental.pallas import tpu as pltpu

F32 = jnp.float32
BF16 = jnp.bfloat16

HEAD_DIM = 64
VAL_DIM = 2 * HEAD_DIM
XHEAD_DIM = 128
MAX_DISTANCE = 128
EPS = 1e-6
QK_SCALE = HEAD_DIM ** -0.5
XSCALE = XHEAD_DIM ** -0.5
NEG = -1e30

LANES = 128
SUBLANES = 8
VMEM_LIMIT_MB = 56

ATTN_TILE = 512
ATTN_STRIP = 256
ROW_TILE = 512
FFN_CHUNK = 256
PROJ_CHUNK = 512

NT_DIMS = (((1,), (1,)), ((), ()))


def _cparams(n_axes):
    return pltpu.CompilerParams(dimension_semantics=("arbitrary",) * n_axes,
                                vmem_limit_bytes=VMEM_LIMIT_MB * 1024 * 1024)


def _resident(block_shape, index_map):
    return pl.BlockSpec(block_shape, index_map, pipeline_mode=pl.Buffered(1))


def _rms(x, g):
    return x * lax.rsqrt(jnp.mean(x * x, axis=-1, keepdims=True) + EPS) * g


def _lambda_init(layer_idx):
    return 0.8 - 0.6 * math.exp(-0.3 * layer_idx)


def _lambda(lamp):
    a = jnp.sum(lamp[0:1] * lamp[1:2], axis=-1, keepdims=True)
    b = jnp.sum(lamp[2:3] * lamp[3:4], axis=-1, keepdims=True)
    return jnp.exp(a) - jnp.exp(b)


def _bucket_np(n, n_buckets):
    max_exact = n_buckets // 2
    nf = np.maximum(n, 1).astype(np.float32)
    large = max_exact + (np.log(nf / np.float32(max_exact)) / np.float32(math.log(MAX_DISTANCE / max_exact))
                         * np.float32(n_buckets - max_exact)).astype(np.int32)
    large = np.minimum(large, n_buckets - 1)
    return np.where(n < max_exact, n, large).astype(np.int32)


def _causal_conv3(z, prev, w):
    row = lax.broadcasted_iota(jnp.int32, z.shape, 0)
    p1 = prev[SUBLANES - 1:SUBLANES, :]
    p2 = prev[SUBLANES - 2:SUBLANES - 1, :]
    z1 = jnp.where(row == 0, p1, pltpu.roll(z, 1, 0))
    z2 = jnp.where(row == 0, p2, jnp.where(row == 1, p1, pltpu.roll(z, 2, 0)))
    return z2 * w[0:1, :] + z1 * w[1:2, :] + z * w[2:3, :]


def _silu(g):
    return g * (1.0 / (1.0 + jnp.exp(-g)))


def _memkv_body(mem_ref, g_ref, w_ref, kf_ref, vf_ref, kb_ref, vb_ref):
    xw = kf_ref.shape[-1]
    xn = _rms(mem_ref[...], g_ref[...]).astype(BF16)
    k = jnp.dot(xn, w_ref[:, :xw], preferred_element_type=F32)
    v = jnp.dot(xn, w_ref[:, xw:], preferred_element_type=F32)
    kf_ref[...] = k
    vf_ref[...] = v
    kb_ref[...] = k.astype(BF16)
    vb_ref[...] = v.astype(BF16)


def _memkv(mem2d, norm_mem, w_mem_bf):
    depth, d, two_xw = w_mem_bf.shape
    xw = two_xw // 2
    rows = mem2d.shape[0]
    out = lambda dt: jax.ShapeDtypeStruct((depth, rows, xw), dt)
    ospec = pl.BlockSpec((None, rows, xw), lambda l: (l, 0, 0))
    return pl.pallas_call(
        _memkv_body,
        grid=(depth,),
        in_specs=[pl.BlockSpec((rows, d), lambda l: (0, 0)),
                  pl.BlockSpec((None, 1, d), lambda l: (l, 0, 0)),
                  pl.BlockSpec((None, d, two_xw), lambda l: (l, 0, 0))],
        out_specs=[ospec, ospec, ospec, ospec],
        out_shape=[out(F32), out(F32), out(BF16), out(BF16)],
        compiler_params=_cparams(1),
        name="mem_kv",
    )(mem2d, norm_mem.reshape(depth, 1, d), w_mem_bf)


def _proj_attn_body(x_ref, g_ref, w_ref, q_ref, kf_ref, vf_ref, kb_ref, vb_ref, qx_ref):
    mw = q_ref.shape[-1]
    xw = qx_ref.shape[-1]
    xn = _rms(x_ref[...], g_ref[...]).astype(BF16)

    def mm(lo, n):
        return jnp.dot(xn, w_ref[:, lo:lo + n], preferred_element_type=F32)

    n = PROJ_CHUNK
    for c in range(0, mw, n):
        q_ref[:, c:c + n] = (mm(c, n) * QK_SCALE).astype(BF16)
        k = mm(mw + c, n)
        kf_ref[:, c:c + n] = k
        kb_ref[:, c:c + n] = k.astype(BF16)
        v = mm(2 * mw + c, n)
        vf_ref[:, c:c + n] = v
        vb_ref[:, c:c + n] = v.astype(BF16)
    qx_ref[...] = mm(3 * mw, xw).astype(BF16)


def _proj_attn(x2d, norm, w_in_bf, layer, tm, mw, xw):
    m, d = x2d.shape
    in_w = w_in_bf.shape[-1]
    row = lambda w: pl.BlockSpec((tm, w), lambda i: (i, 0))
    return pl.pallas_call(
        _proj_attn_body,
        grid=(m // tm,),
        in_specs=[row(d),
                  pl.BlockSpec((None, 1, d), lambda i: (layer, 0, 0)),
                  _resident((None, d, in_w), lambda i: (layer, 0, 0))],
        out_specs=[row(mw), row(mw), row(mw), row(mw), row(mw), row(xw)],
        out_shape=[jax.ShapeDtypeStruct((m, mw), BF16),
                   jax.ShapeDtypeStruct((m, mw), F32),
                   jax.ShapeDtypeStruct((m, mw), F32),
                   jax.ShapeDtypeStruct((m, mw), BF16),
                   jax.ShapeDtypeStruct((m, mw), BF16),
                   jax.ShapeDtypeStruct((m, xw), BF16)],
        compiler_params=_cparams(1),
        name="proj_attn",
    )(x2d, norm, w_in_bf)


def _proj_conv_body(x_ref, g_ref, w_ref, cw_ref, mix_ref, qx_ref, st_ref, *, tiles_per_seq):
    mw = mix_ref.shape[-1]
    xw = qx_ref.shape[-1]
    tm = x_ref.shape[0]
    first = (pl.program_id(0) % tiles_per_seq) == 0
    xn = _rms(x_ref[...], g_ref[...]).astype(BF16)

    def mm(lo, n):
        return jnp.dot(xn, w_ref[:, lo:lo + n], preferred_element_type=F32)

    n = PROJ_CHUNK
    for c in range(0, mw, n):
        z = mm(mw + c, n) * mm(2 * mw + c, n)
        prev = jnp.where(first, 0.0, st_ref[:, c:c + n])
        y = _causal_conv3(z, prev, cw_ref[:, c:c + n])
        st_ref[:, c:c + n] = z[tm - SUBLANES:, :]
        mix_ref[:, c:c + n] = (mm(c, n) * y).astype(BF16)
    qx_ref[...] = mm(3 * mw, xw).astype(BF16)


def _proj_conv(x2d, norm, w_in_bf, conv_w, layer, conv_layer, tm, seq, mw, xw):
    m, d = x2d.shape
    in_w = w_in_bf.shape[-1]
    tps = seq // tm
    row = lambda w: pl.BlockSpec((tm, w), lambda i: (i, 0))
    return pl.pallas_call(
        functools.partial(_proj_conv_body, tiles_per_seq=tps),
        grid=(m // tm,),
        in_specs=[row(d),
                  pl.BlockSpec((None, 1, d), lambda i: (layer, 0, 0)),
                  _resident((None, d, in_w), lambda i: (layer, 0, 0)),
                  pl.BlockSpec((None, conv_w.shape[1], mw), lambda i: (conv_layer, 0, 0))],
        out_specs=[row(mw), row(xw),
                   pl.BlockSpec((None, SUBLANES, mw), lambda i: (i // tps, 0, 0))],
        out_shape=[jax.ShapeDtypeStruct((m, mw), BF16),
                   jax.ShapeDtypeStruct((m, xw), BF16),
                   jax.ShapeDtypeStruct((m // seq, SUBLANES, mw), F32)],
        compiler_params=_cparams(1),
        name="proj_conv",
    )(x2d, norm, w_in_bf, conv_w)


def _attn_body(tab_ref, q_ref, k_ref, vt_ref, bk_ref, lamp_ref, g_ref, o_ref,
               acc1, acc2, mrun, bias2, sa1, sa2, sb1, sb2, bma, bmb, *, lam_init):
    t = q_ref.shape[0]
    n_buckets = tab_ref.shape[0]
    h = pl.program_id(1)
    qi = pl.program_id(2)
    nb = t // LANES

    @pl.when(qi == 0)
    def _build_bias():
        far = tab_ref[n_buckets - 1, h]

        def tile(bk):
            out = jnp.zeros(bk.shape, F32)
            for b in range(n_buckets - 1):
                out = jnp.where(bk == b, tab_ref[b, h] - far, out)
            return jnp.where(bk < 0, NEG, out)

        d0 = tile(bk_ref[0])
        d1 = tile(bk_ref[1])
        bias2[...] = jnp.zeros(bias2.shape, F32)
        bias2[0, (nb - 1) * LANES:, 0:LANES] = d1
        for r in range(nb):
            for c in range(nb):
                sub = (1, slice(r * LANES, (r + 1) * LANES), slice(c * LANES, (c + 1) * LANES))
                if c < r:
                    bias2[sub] = jnp.full((LANES, LANES), NEG, F32)
                elif c == r:
                    bias2[sub] = d0
                elif c == r + 1:
                    bias2[sub] = d1

    q = q_ref[...]
    lane = lax.broadcasted_iota(jnp.int32, q.shape, 1)
    zero = jnp.zeros_like(q)
    qz = (jnp.where(lane < HEAD_DIM, q, zero), jnp.where(lane >= HEAD_DIM, q, zero))
    accs = (acc1, acc2)
    ones = jnp.ones((2 * SUBLANES, t), BF16)

    acc1[...] = jnp.zeros(acc1.shape, F32)
    acc2[...] = jnp.zeros(acc2.shape, F32)
    mrun[...] = jnp.full(mrun.shape, NEG, F32)

    def logits_stage(j, buf, bias_idx=None):
        sbufs, bm = buf
        kb = k_ref[pl.ds(pl.multiple_of(j * t, t), t), :]
        for comp in range(2):
            s = lax.dot_general(kb, qz[comp], NT_DIMS, preferred_element_type=F32)
            if bias_idx is not None:
                s = s + bias2[bias_idx]
            sbufs[comp][...] = s
            bm[comp:comp + 1, :] = jnp.max(s, axis=0, keepdims=True)

    def softmax_stage(j, buf):
        sbufs, bm = buf
        vt = jnp.concatenate([vt_ref[j], ones], axis=0)
        for comp in range(2):
            m_old = mrun[comp:comp + 1, :]
            m_new = jnp.maximum(m_old, bm[comp:comp + 1, :])
            alpha = jnp.exp(m_old - m_new)
            mrun[comp:comp + 1, :] = m_new
            for c in range(0, t, ATTN_STRIP):
                cs = slice(c, c + ATTN_STRIP)
                p = jnp.exp(sbufs[comp][:, cs] - m_new[:, cs]).astype(BF16)
                accs[comp][:, cs] = (alpha[:, cs] * accs[comp][:, cs]
                                     + jnp.dot(vt, p, preferred_element_type=F32))

    buf_a = ((sa1, sa2), bma)
    buf_b = ((sb1, sb2), bmb)
    PREV, DIAG = 0, 1

    @pl.when(qi == 0)
    def _one_block():
        logits_stage(0, buf_a, DIAG)
        softmax_stage(0, buf_a)

    @pl.when(qi == 1)
    def _two_blocks():
        logits_stage(0, buf_a, PREV)
        logits_stage(1, buf_b, DIAG)
        softmax_stage(0, buf_a)
        softmax_stage(1, buf_b)

    @pl.when(qi >= 2)
    def _bias_free_blocks():
        n_pairs = (qi - 2) // 2
        logits_stage(0, buf_a)

        def pair(i, carry):
            j = 2 * i
            logits_stage(j + 1, buf_b)
            softmax_stage(j, buf_a)
            logits_stage(j + 2, buf_a)
            softmax_stage(j + 1, buf_b)
            return carry

        lax.fori_loop(0, n_pairs, pair, 0)

    @pl.when((qi >= 2) & (qi % 2 == 0))
    def _even_tail():
        logits_stage(qi - 1, buf_b, PREV)
        softmax_stage(qi - 2, buf_a)
        logits_stage(qi, buf_a, DIAG)
        softmax_stage(qi - 1, buf_b)
        softmax_stage(qi, buf_a)

    @pl.when((qi >= 2) & (qi % 2 == 1))
    def _odd_tail():
        logits_stage(qi - 2, buf_b)
        softmax_stage(qi - 3, buf_a)
        logits_stage(qi - 1, buf_a, PREV)
        softmax_stage(qi - 2, buf_b)
        logits_stage(qi, buf_b, DIAG)
        softmax_stage(qi - 1, buf_a)
        softmax_stage(qi, buf_b)

    lam = _lambda(lamp_ref[...]) + lam_init
    inv1 = 1.0 / acc1[VAL_DIM:VAL_DIM + 1, :]
    inv2 = 1.0 / acc2[VAL_DIM:VAL_DIM + 1, :]
    o_t = acc1[0:VAL_DIM, :] * inv1 - lam * (acc2[0:VAL_DIM, :] * inv2)
    o = o_t.T
    y = o * lax.rsqrt(jnp.mean(o * o, axis=-1, keepdims=True) + EPS) * g_ref[...] * (1.0 - lam_init)
    o_ref[...] = y.astype(o_ref.dtype)


def _attn_bucket_tiles(n_buckets):
    kk = np.arange(LANES)[:, None]
    qq = np.arange(LANES)[None, :]
    d0 = qq - kk
    t0 = np.where(d0 >= 0, _bucket_np(np.maximum(d0, 0), n_buckets), -1)
    t1 = _bucket_np(LANES + qq - kk, n_buckets)
    return np.stack([t0, t1]).astype(np.int32)


def _attention(q_bf, k_bf, vt_bf, rel_bias, lamp, gain, batch, seq, n_heads, layer):
    t = ATTN_TILE
    nq = seq // t
    assert seq % t == 0 and t >= 2 * LANES and MAX_DISTANCE <= LANES
    m, mw = q_bf.shape
    bk = jnp.asarray(_attn_bucket_tiles(rel_bias.shape[0]))
    return pl.pallas_call(
        functools.partial(_attn_body, lam_init=_lambda_init(layer)),
        grid=(batch, n_heads, nq),
        in_specs=[pl.BlockSpec(memory_space=pltpu.SMEM),
                  pl.BlockSpec((t, VAL_DIM), lambda b, h, i: (b * nq + i, h)),
                  pl.BlockSpec((seq, VAL_DIM), lambda b, h, i: (b, h)),
                  pl.BlockSpec((None, None, nq, VAL_DIM, t), lambda b, h, i: (b, h, 0, 0, 0)),
                  pl.BlockSpec((2, LANES, LANES), lambda b, h, i: (0, 0, 0)),
                  pl.BlockSpec((4, HEAD_DIM), lambda b, h, i: (0, 0)),
                  pl.BlockSpec((1, VAL_DIM), lambda b, h, i: (0, 0))],
        out_specs=pl.BlockSpec((t, VAL_DIM), lambda b, h, i: (b * nq + i, h)),
        out_shape=jax.ShapeDtypeStruct((m, mw), BF16),
        scratch_shapes=[pltpu.VMEM((VAL_DIM + 2 * SUBLANES, t), F32),
                        pltpu.VMEM((VAL_DIM + 2 * SUBLANES, t), F32),
                        pltpu.VMEM((SUBLANES, t), F32),
                        pltpu.VMEM((2, t, t), F32),
                        pltpu.VMEM((t, t), F32), pltpu.VMEM((t, t), F32),
                        pltpu.VMEM((t, t), F32), pltpu.VMEM((t, t), F32),
                        pltpu.VMEM((SUBLANES, t), F32), pltpu.VMEM((SUBLANES, t), F32)],
        compiler_params=_cparams(3),
        name="diff_attn",
    )(rel_bias, q_bf, k_bf, vt_bf, bk, lamp, gain)


def _out_body(mix_ref, qx_ref, mk_ref, mv_ref, w_ref, x_ref, o_ref):
    mw = mix_ref.shape[-1]
    xw = qx_ref.shape[-1]
    acc = jnp.dot(mix_ref[...], w_ref[0:mw, :], preferred_element_type=F32)
    for c in range(0, xw, XHEAD_DIM):
        hs = slice(c, c + XHEAD_DIM)
        s = lax.dot_general(qx_ref[:, hs], mk_ref[:, hs], NT_DIMS, preferred_element_type=F32) * XSCALE
        p = jnp.exp(s - jnp.max(s, axis=-1, keepdims=True))
        inv = 1.0 / jnp.sum(p, axis=-1, keepdims=True)
        xa = jnp.dot(p.astype(BF16), mv_ref[:, hs], preferred_element_type=F32) * inv
        acc = acc + jnp.dot(xa.astype(BF16), w_ref[mw + c:mw + c + XHEAD_DIM, :],
                            preferred_element_type=F32)
    o_ref[...] = x_ref[...] + acc


def _out_proj(mix_bf, qx_bf, mk_bf, mv_bf, w_out_bf, x2d, layer, tm, seq):
    m, d = x2d.shape
    mw = mix_bf.shape[-1]
    xw = qx_bf.shape[-1]
    n_mem = mk_bf.shape[1] // (m // seq)
    tps = seq // tm
    row = lambda w: pl.BlockSpec((tm, w), lambda i: (i, 0))
    mem = pl.BlockSpec((None, n_mem, xw), lambda i: (layer, i // tps, 0))
    return pl.pallas_call(
        _out_body,
        grid=(m // tm,),
        in_specs=[row(mw), row(xw), mem, mem,
                  _resident((None, mw + xw, d), lambda i: (layer, 0, 0)),
                  row(d)],
        out_specs=row(d),
        out_shape=jax.ShapeDtypeStruct((m, d), F32),
        compiler_params=_cparams(1),
        name="out_proj",
    )(mix_bf, qx_bf, mk_bf, mv_bf, w_out_bf, x2d)


def _ffn_body(x_ref, g_ref, wu_ref, cw_ref, wd_ref, gf_ref, o_ref, st_ref, act_ref,
              *, tiles_per_seq, final_norm):
    tm = x_ref.shape[0]
    dff = wd_ref.shape[0]
    first = (pl.program_id(0) % tiles_per_seq) == 0
    x = x_ref[...]
    xn = _rms(x, g_ref[...]).astype(BF16)

    def conv_cols(lo, n):
        h = jnp.dot(xn, wu_ref[:, lo:lo + n], preferred_element_type=F32)
        prev = jnp.where(first, 0.0, st_ref[:, lo:lo + n])
        st_ref[:, lo:lo + n] = h[tm - SUBLANES:, :]
        return _causal_conv3(h, prev, cw_ref[:, lo:lo + n])

    n = FFN_CHUNK
    for c in range(0, dff, n):
        act_ref[:, c:c + n] = (_silu(conv_cols(c, n)) * conv_cols(dff + c, n)).astype(BF16)
    y = x + jnp.dot(act_ref[...], wd_ref[...], preferred_element_type=F32)
    if final_norm:
        y = _rms(y, gf_ref[...])
    o_ref[...] = y


def _ffn(x2d, norm, w_up_bf, ffn_conv_w, w_down_bf, norm_final, layer, tm, seq, final_norm):
    m, d = x2d.shape
    dff = w_down_bf.shape[1]
    assert dff % FFN_CHUNK == 0
    tps = seq // tm
    row = pl.BlockSpec((tm, d), lambda i: (i, 0))
    return pl.pallas_call(
        functools.partial(_ffn_body, tiles_per_seq=tps, final_norm=final_norm),
        grid=(m // tm,),
        in_specs=[row,
                  pl.BlockSpec((None, 1, d), lambda i: (layer, 0, 0)),
                  _resident((None, d, 2 * dff), lambda i: (layer, 0, 0)),
                  pl.BlockSpec((None, ffn_conv_w.shape[1], 2 * dff), lambda i: (layer, 0, 0)),
                  _resident((None, dff, d), lambda i: (layer, 0, 0)),
                  pl.BlockSpec((1, d), lambda i: (0, 0))],
        out_specs=[row, pl.BlockSpec((None, SUBLANES, 2 * dff), lambda i: (i // tps, 0, 0))],
        out_shape=[jax.ShapeDtypeStruct((m, d), F32),
                   jax.ShapeDtypeStruct((m // seq, SUBLANES, 2 * dff), F32)],
        scratch_shapes=[pltpu.VMEM((tm, dff), BF16)],
        compiler_params=_cparams(1),
        name="conv_ffn",
    )(x2d, norm, w_up_bf, ffn_conv_w, w_down_bf, norm_final)


def _proj_plain_body(x_ref, g_ref, w_ref, u_ref):
    xn = _rms(x_ref[...], g_ref[...]).astype(BF16)
    u_ref[...] = jnp.dot(xn, w_ref[...], preferred_element_type=F32)


def _proj_plain(x2d, norm, w_in_bf, layer):
    m, d = x2d.shape
    in_w = w_in_bf.shape[-1]
    n = PROJ_CHUNK
    return pl.pallas_call(
        _proj_plain_body,
        grid=(in_w // n,),
        in_specs=[pl.BlockSpec((m, d), lambda c: (0, 0)),
                  pl.BlockSpec((None, 1, d), lambda c: (layer, 0, 0)),
                  pl.BlockSpec((None, d, n), lambda c: (layer, 0, c))],
        out_specs=pl.BlockSpec((m, n), lambda c: (0, c)),
        out_shape=jax.ShapeDtypeStruct((m, in_w), F32),
        compiler_params=_cparams(1),
        name="proj_sample",
    )(x2d, norm, w_in_bf)


def _paged_attn_body(pt_ref, q_ref, kn_ref, vn_ref, bks_ref, tab_ref, lamp_ref, g_ref, *rest,
                     n_pages, lam_init):
    k_refs = rest[:n_pages]
    v_refs = rest[n_pages:2 * n_pages]
    o_ref, s_ref, bias_ref = rest[2 * n_pages:]
    rows = k_refs[0].shape[0]
    n_heads = kn_ref.shape[0]
    past = n_pages * rows
    n_buckets = tab_ref.shape[1]

    @pl.when(pl.program_id(0) == 0)
    def _build_bias():
        bk = bks_ref[...]
        out = jnp.zeros(bk.shape, F32)
        for b in range(n_buckets):
            out = jnp.where(bk == b, tab_ref[:, b:b + 1], out)
        bias_ref[...] = jnp.where(bk < 0, NEG, out)

    q = q_ref[...]
    for r in range(n_pages):
        s_ref[:, r * rows:(r + 1) * rows] = lax.dot_general(
            q, k_refs[r][...], NT_DIMS, preferred_element_type=F32)
    pad = jnp.zeros((LANES - n_heads, VAL_DIM), F32)
    s_ref[:, past:] = lax.dot_general(q, jnp.concatenate([kn_ref[...], pad], axis=0), NT_DIMS,
                                      preferred_element_type=F32)

    s = s_ref[...] + bias_ref[...]
    p = jnp.exp(s - jnp.max(s, axis=-1, keepdims=True))
    pn = p * (1.0 / jnp.sum(p, axis=-1, keepdims=True))
    pv = jnp.dot(pn[:, past:], jnp.concatenate([vn_ref[...], pad], axis=0), preferred_element_type=F32)
    for r in range(n_pages):
        pv = pv + jnp.dot(pn[:, r * rows:(r + 1) * rows], v_refs[r][...], preferred_element_type=F32)
    lam = _lambda(lamp_ref[...]) + lam_init
    o = pv[0:n_heads, :] - lam * pv[n_heads:, :]
    o_ref[...] = o * lax.rsqrt(jnp.mean(o * o, axis=-1, keepdims=True) + EPS) * g_ref[...] * (1.0 - lam_init)


def _paged_attention(q, k_new, v_new, cache_k, cache_v, page_table, rel_bias, lamp, gain, layer, attn_layer):
    db, width = q.shape
    n_layers, n_phys, page, n_heads, vd = cache_k.shape
    n_pages = page_table.shape[1]
    n_buckets = rel_bias.shape[0]
    rows = page * n_heads
    past = n_pages * rows
    assert vd == VAL_DIM == LANES and n_heads <= LANES
    ck = cache_k.reshape(n_layers, n_phys, rows, vd)
    cv = cache_v.reshape(n_layers, n_phys, rows, vd)
    q3 = q.reshape(db, 1, n_heads, vd) * QK_SCALE
    comp = (np.arange(vd) // HEAD_DIM)[None, None, None, :] == np.arange(2)[None, :, None, None]
    qrows = jnp.where(comp, q3, 0.0).reshape(db, 2 * n_heads, vd)
    col = np.arange(past + LANES)
    key = np.minimum(col // n_heads, past // n_heads)
    valid = (col < past + n_heads)[None, :] & ((col % n_heads)[None, :]
                                               == (np.arange(2 * n_heads) % n_heads)[:, None])
    bucket = _bucket_np(past // n_heads - key, n_buckets)
    bks = jnp.asarray(np.where(valid, bucket[None, :], -1).astype(np.int32))
    tab = jnp.tile(rel_bias.T, (2, 1))
    head_rows = lambda a: a.reshape(db, n_heads, vd)
    hspec = pl.BlockSpec((None, n_heads, vd), lambda b, pt: (b, 0, 0))

    def page_spec(r):
        return pl.BlockSpec((None, None, rows, vd), lambda b, pt: (attn_layer, pt[b, r], 0, 0))

    const = lambda shape: pl.BlockSpec(shape, lambda b, pt: (0,) * len(shape))
    grid_spec = pltpu.PrefetchScalarGridSpec(
        num_scalar_prefetch=1,
        grid=(db,),
        in_specs=[pl.BlockSpec((None, 2 * n_heads, vd), lambda b, pt: (b, 0, 0)), hspec, hspec,
                  const(bks.shape), const(tab.shape), const((4, HEAD_DIM)), const((1, VAL_DIM))]
                 + [page_spec(r) for r in range(n_pages)] * 2,
        out_specs=hspec,
        scratch_shapes=[pltpu.VMEM(bks.shape, F32), pltpu.VMEM(bks.shape, F32)],
    )
    out = pl.pallas_call(
        functools.partial(_paged_attn_body, n_pages=n_pages, lam_init=_lambda_init(layer)),
        grid_spec=grid_spec,
        out_shape=jax.ShapeDtypeStruct((db, n_heads, vd), F32),
        compiler_params=_cparams(1),
        name="paged_attn",
    )(page_table, qrows, head_rows(k_new), head_rows(v_new), bks, tab, lamp, gain,
      *([ck] * n_pages), *([cv] * n_pages))
    return out.reshape(db, width)


def _mem_sample_body(qx_ref, mk_ref, mv_ref, o_ref, *, n_xheads):
    g, rows, _ = mk_ref.shape
    col = lax.broadcasted_iota(jnp.int32, (SUBLANES, rows), 1)
    row = lax.broadcasted_iota(jnp.int32, (SUBLANES, rows), 0)
    own = col % n_xheads == row
    for b in range(g):
        s = lax.dot_general(qx_ref[b], mk_ref[b], NT_DIMS, preferred_element_type=F32) * XSCALE
        s = jnp.where(own, s, NEG)
        p = jnp.exp(s - jnp.max(s, axis=-1, keepdims=True))
        pn = p * (1.0 / jnp.sum(p, axis=-1, keepdims=True))
        o_ref[b] = jnp.dot(pn, mv_ref[b], preferred_element_type=F32)


def _mem_sample(qx, cache_mem_k, cache_mem_v, layer):
    db, xw = qx.shape
    depth, _, n_mem, n_xheads, xd = cache_mem_k.shape
    assert xd == XHEAD_DIM == LANES and n_xheads <= SUBLANES
    g = SUBLANES
    rows = n_mem * n_xheads
    mk = cache_mem_k.reshape(depth, db, rows, xd)
    mv = cache_mem_v.reshape(depth, db, rows, xd)
    qx8 = jnp.pad(qx.reshape(db, n_xheads, xd), ((0, 0), (0, SUBLANES - n_xheads), (0, 0)))
    mem = pl.BlockSpec((None, g, rows, xd), lambda i: (layer, i, 0, 0))
    vec = pl.BlockSpec((g, SUBLANES, xd), lambda i: (i, 0, 0))
    out = pl.pallas_call(
        functools.partial(_mem_sample_body, n_xheads=n_xheads),
        grid=(db // g,),
        in_specs=[vec, mem, mem],
        out_specs=vec,
        out_shape=jax.ShapeDtypeStruct((db, SUBLANES, xd), F32),
        compiler_params=_cparams(1),
        name="mem_attn_sample",
    )(qx8, mk, mv)
    return out[:, :n_xheads, :].reshape(db, xw)


def _out_sample_attn_body(mix_ref, xa_ref, w_ref, x_ref, y_ref):
    mw = mix_ref.shape[-1]
    acc = jnp.dot(mix_ref[...].astype(BF16), w_ref[0:mw, :], preferred_element_type=F32)
    acc = acc + jnp.dot(xa_ref[...].astype(BF16), w_ref[mw:, :], preferred_element_type=F32)
    y_ref[...] = x_ref[...] + acc


def _out_sample_conv_body(u_ref, st_ref, cw_ref, xa_ref, w_ref, x_ref, y_ref, z_ref):
    mw = z_ref.shape[-1]
    z = u_ref[:, mw:2 * mw] * u_ref[:, 2 * mw:3 * mw]
    conv = st_ref[:, 0:mw] * cw_ref[0:1, :] + st_ref[:, mw:] * cw_ref[1:2, :] + z * cw_ref[2:3, :]
    mix = u_ref[:, 0:mw] * conv
    z_ref[...] = z
    acc = jnp.dot(mix.astype(BF16), w_ref[0:mw, :], preferred_element_type=F32)
    acc = acc + jnp.dot(xa_ref[...].astype(BF16), w_ref[mw:, :], preferred_element_type=F32)
    y_ref[...] = x_ref[...] + acc


def _full(a):
    return pl.BlockSpec(a.shape, lambda i: (0,) * a.ndim)


def _out_sample_attn(mix, xa, w_out_bf, x2d, layer):
    w_spec = pl.BlockSpec((None,) + w_out_bf.shape[1:], lambda i: (layer, 0, 0))
    return pl.pallas_call(
        _out_sample_attn_body,
        grid=(1,),
        in_specs=[_full(mix), _full(xa), w_spec, _full(x2d)],
        out_specs=_full(x2d),
        out_shape=jax.ShapeDtypeStruct(x2d.shape, F32),
        compiler_params=_cparams(1),
        name="out_sample_attn",
    )(mix, xa, w_out_bf, x2d)


def _out_sample_conv(u, state2d, conv_w, xa, w_out_bf, x2d, layer, conv_layer, mw):
    db = u.shape[0]
    w_spec = pl.BlockSpec((None,) + w_out_bf.shape[1:], lambda i: (layer, 0, 0))
    cw_spec = pl.BlockSpec((None,) + conv_w.shape[1:], lambda i: (conv_layer, 0, 0))
    z_shape = jax.ShapeDtypeStruct((db, mw), F32)
    return pl.pallas_call(
        _out_sample_conv_body,
        grid=(1,),
        in_specs=[_full(u), _full(state2d), cw_spec, _full(xa), w_spec, _full(x2d)],
        out_specs=[_full(x2d), _full(z_shape)],
        out_shape=[jax.ShapeDtypeStruct(x2d.shape, F32), z_shape],
        compiler_params=_cparams(1),
        name="out_sample_conv",
    )(u, state2d, conv_w, xa, w_out_bf, x2d)


def _ffn_sample_body(x_ref, g_ref, wg_ref, wu_ref, cg_ref, cu_ref, s0g_ref, s0u_ref, s1g_ref, s1u_ref,
                     wd_ref, gf_ref, y_ref, hg_ref, hu_ref, xn_ref, *, final_norm):
    c = pl.program_id(0)

    @pl.when(c == 0)
    def _init():
        xn_ref[...] = _rms(x_ref[...], g_ref[...]).astype(BF16)
        y_ref[...] = x_ref[...]

    def step(w_ref, cw_ref, s0_ref, s1_ref, h_ref):
        h = jnp.dot(xn_ref[...], w_ref[...], preferred_element_type=F32)
        h_ref[...] = h
        return s0_ref[...] * cw_ref[0:1, :] + s1_ref[...] * cw_ref[1:2, :] + h * cw_ref[2:3, :]

    gate = step(wg_ref, cg_ref, s0g_ref, s1g_ref, hg_ref)
    up = step(wu_ref, cu_ref, s0u_ref, s1u_ref, hu_ref)
    act = (_silu(gate) * up).astype(BF16)
    y_ref[...] += jnp.dot(act, wd_ref[...], preferred_element_type=F32)

    if final_norm:
        @pl.when(c == pl.num_programs(0) - 1)
        def _finish():
            y_ref[...] = _rms(y_ref[...], gf_ref[...])


def _ffn_sample(x2d, norm, w_up_bf, ffn_conv_w, w_down_bf, state2d, norm_final, layer, final_norm):
    m, d = x2d.shape
    dff = w_down_bf.shape[1]
    n = FFN_CHUNK
    nch = dff // n
    kw = ffn_conv_w.shape[1]
    xs = pl.BlockSpec((m, d), lambda c: (0, 0))
    col = lambda off: pl.BlockSpec((m, n), lambda c: (0, off + c))
    return pl.pallas_call(
        functools.partial(_ffn_sample_body, final_norm=final_norm),
        grid=(nch,),
        in_specs=[xs,
                  pl.BlockSpec((None, 1, d), lambda c: (layer, 0, 0)),
                  pl.BlockSpec((None, d, n), lambda c: (layer, 0, c)),
                  pl.BlockSpec((None, d, n), lambda c: (layer, 0, nch + c)),
                  pl.BlockSpec((None, kw, n), lambda c: (layer, 0, c)),
                  pl.BlockSpec((None, kw, n), lambda c: (layer, 0, nch + c)),
                  col(0), col(nch), col(2 * nch), col(3 * nch),
                  pl.BlockSpec((None, n, d), lambda c: (layer, c, 0)),
                  pl.BlockSpec((1, d), lambda c: (0, 0))],
        out_specs=[xs, col(0), col(0)],
        out_shape=[jax.ShapeDtypeStruct((m, d), F32),
                   jax.ShapeDtypeStruct((m, dff), F32),
                   jax.ShapeDtypeStruct((m, dff), F32)],
        scratch_shapes=[pltpu.VMEM((m, d), BF16)],
        compiler_params=_cparams(1),
        name="conv_ffn_sample",
    )(x2d, norm, w_up_bf, w_up_bf, ffn_conv_w, ffn_conv_w, state2d, state2d, state2d, state2d,
      w_down_bf, norm_final)


def kernel(x_prompt, x_sample, mem_prompt, cache_k, cache_v, page_table, cache_mem_k, cache_mem_v,
           state_conv, state_ffn, rel_bias, w_in, w_out, norm_mix, norm_mem, w_mem_kv,
           lambda_q1, lambda_k1, lambda_q2, lambda_k2, subln_gain, conv_w,
           norm_ffn, w_up, ffn_conv_w, w_down, norm_final):
    batch, seq, d = x_prompt.shape
    db = x_sample.shape[0]
    depth = w_in.shape[0]
    n_mem = mem_prompt.shape[1]
    n_heads = cache_k.shape[3]
    mw = n_heads * VAL_DIM
    xw = w_mem_kv.shape[-1] // 2
    n_xheads = xw // XHEAD_DIM
    dff = w_down.shape[1]
    tm = min(ROW_TILE, seq)
    assert x_sample.shape[1] == 1 and seq % tm == 0 and w_in.shape[-1] == 3 * mw + xw

    w_in_bf = w_in.astype(BF16)
    w_out_bf = w_out.astype(BF16)
    w_up_bf = w_up.astype(BF16)
    w_down_bf = w_down.astype(BF16)
    w_mem_bf = w_mem_kv.astype(BF16)
    norm_mix3 = norm_mix.reshape(depth, 1, d)
    norm_ffn3 = norm_ffn.reshape(depth, 1, d)
    norm_final2 = norm_final.reshape(1, d)
    nt = seq // ATTN_TILE

    xp = x_prompt.reshape(batch * seq, d)
    xs = x_sample.reshape(db, d)

    mk_f, mv_f, mk_bf, mv_bf = _memkv(mem_prompt.reshape(batch * n_mem, d), norm_mem, w_mem_bf)

    k_rows_p, v_rows_p, k_rows_s, v_rows_s = [], [], [], []
    conv_st_p, conv_st_s, ffn_st_p, ffn_st_s = [], [], [], []

    for i in range(depth):
        li = i // 2
        last = i == depth - 1
        if i % 2 == 0:
            lamp = jnp.stack([lambda_q1[li], lambda_k1[li], lambda_q2[li], lambda_k2[li]])
            gain = subln_gain[li].reshape(1, VAL_DIM)
            q_bf, k_f, v_f, k_bf, v_bf, qx_bf = _proj_attn(xp, norm_mix3, w_in_bf, i, tm, mw, xw)
            vt_bf = v_bf.reshape(batch, nt, ATTN_TILE, n_heads, VAL_DIM).transpose(0, 3, 1, 4, 2)
            mix_bf = _attention(q_bf, k_bf, vt_bf, rel_bias, lamp, gain, batch, seq, n_heads, i)
            k_rows_p.append(k_f.reshape(batch, seq, n_heads, VAL_DIM))
            v_rows_p.append(v_f.reshape(batch, seq, n_heads, VAL_DIM))
            us = _proj_plain(xs, norm_mix3, w_in_bf, i)
            qs, ks, vs = us[:, :mw], us[:, mw:2 * mw], us[:, 2 * mw:3 * mw]
            mix_s = _paged_attention(qs, ks, vs, cache_k, cache_v, page_table, rel_bias, lamp, gain, i, li)
            xa_s = _mem_sample(us[:, 3 * mw:], cache_mem_k, cache_mem_v, i)
            xs = _out_sample_attn(mix_s, xa_s, w_out_bf, xs, i)
            k_rows_s.append(ks.reshape(db, 1, n_heads, VAL_DIM))
            v_rows_s.append(vs.reshape(db, 1, n_heads, VAL_DIM))
        else:
            mix_bf, qx_bf, cst = _proj_conv(xp, norm_mix3, w_in_bf, conv_w, i, li, tm, seq, mw, xw)
            conv_st_p.append(cst[:, SUBLANES - (conv_w.shape[1] - 1):, :])
            us = _proj_plain(xs, norm_mix3, w_in_bf, i)
            xa_s = _mem_sample(us[:, 3 * mw:], cache_mem_k, cache_mem_v, i)
            st = state_conv[li]
            xs, z_s = _out_sample_conv(us, st.reshape(db, -1), conv_w, xa_s, w_out_bf, xs, i, li, mw)
            conv_st_s.append(jnp.stack([st[:, 1, :], z_s], axis=1))

        xp = _out_proj(mix_bf, qx_bf, mk_bf, mv_bf, w_out_bf, xp, i, tm, seq)
        xp, fst = _ffn(xp, norm_ffn3, w_up_bf, ffn_conv_w, w_down_bf, norm_final2, i, tm, seq, last)
        ffn_st_p.append(fst[:, SUBLANES - (ffn_conv_w.shape[1] - 1):, :])

        fs = state_ffn[i]
        xs, hg, hu = _ffn_sample(xs, norm_ffn3, w_up_bf, ffn_conv_w, w_down_bf, fs.reshape(db, -1),
                                 norm_final2, i, last)
        ffn_st_s.append(jnp.stack([fs[:, 1, :], jnp.concatenate([hg, hu], axis=-1)], axis=1))

    mem_shape = (depth, batch, n_mem, n_xheads, XHEAD_DIM)
    return (xp.reshape(batch, seq, d), xs.reshape(db, 1, d),
            jnp.stack(k_rows_p), jnp.stack(v_rows_p), jnp.stack(k_rows_s), jnp.stack(v_rows_s),
            mk_f.reshape(mem_shape), mv_f.reshape(mem_shape),
            jnp.stack(conv_st_p), jnp.stack(conv_st_s), jnp.stack(ffn_st_p), jnp.stack(ffn_st_s))
```

```python
import functools
import math

import numpy as np
import jax
import jax.numpy as jnp
from jax import lax
from jax.experimental import pallas as pl
from jax.experimental.pallas import tpu as pltpu

F32 = jnp.float32
BF16 = jnp.bfloat16

HEAD_DIM = 64
VAL_DIM = 2 * HEAD_DIM
XHEAD_DIM = 128
MAX_DISTANCE = 128
EPS = 1e-6
QK_SCALE = HEAD_DIM ** -0.5
XSCALE = XHEAD_DIM ** -0.5
LOG2E = math.log2(math.e)
NEG = -1e30

LANES = 128
SUBLANES = 8
VMEM_LIMIT_MB = 56

ATTN_TILE = 512
ATTN_STRIP = 256
ATTN_HEADS = 2
ROW_TILE = 512
FFN_CHUNK = 256
PROJ_CHUNK = 512

NT_DIMS = (((1,), (1,)), ((), ()))


def _cparams(n_axes):
    return pltpu.CompilerParams(dimension_semantics=("arbitrary",) * n_axes,
                                vmem_limit_bytes=VMEM_LIMIT_MB * 1024 * 1024)


def _resident(block_shape, index_map):
    return pl.BlockSpec(block_shape, index_map, pipeline_mode=pl.Buffered(1))


def _rms(x, g):
    return x * lax.rsqrt(jnp.mean(x * x, axis=-1, keepdims=True) + EPS) * g


def _lambda_init(layer_idx):
    return 0.8 - 0.6 * math.exp(-0.3 * layer_idx)


def _lambda(lamp):
    a = jnp.sum(lamp[0:1] * lamp[1:2], axis=-1, keepdims=True)
    b = jnp.sum(lamp[2:3] * lamp[3:4], axis=-1, keepdims=True)
    return jnp.exp(a) - jnp.exp(b)


def _bucket_np(n, n_buckets):
    max_exact = n_buckets // 2
    nf = np.maximum(n, 1).astype(np.float32)
    large = max_exact + (np.log(nf / np.float32(max_exact)) / np.float32(math.log(MAX_DISTANCE / max_exact))
                         * np.float32(n_buckets - max_exact)).astype(np.int32)
    large = np.minimum(large, n_buckets - 1)
    return np.where(n < max_exact, n, large).astype(np.int32)


def _causal_conv3(z, prev, w):
    def taps(a):
        return pltpu.roll(a, 2, 0) * w[0:1, :] + pltpu.roll(a, 1, 0) * w[1:2, :] + a * w[2:3, :]

    head = taps(jnp.concatenate([prev, z[0:SUBLANES, :]], axis=0))[SUBLANES:, :]
    return jnp.concatenate([head, taps(z)[SUBLANES:, :]], axis=0)


def _silu(g):
    return g * (1.0 / (1.0 + jnp.exp(-g)))


def _memkv_body(mem_ref, g_ref, w_ref, kf_ref, vf_ref, kb_ref, vb_ref):
    xw = kf_ref.shape[-1]
    xn = _rms(mem_ref[...], g_ref[...]).astype(BF16)
    k = jnp.dot(xn, w_ref[:, :xw], preferred_element_type=F32)
    v = jnp.dot(xn, w_ref[:, xw:], preferred_element_type=F32)
    kf_ref[...] = k
    vf_ref[...] = v
    kb_ref[...] = k.astype(BF16)
    vb_ref[...] = v.astype(BF16)


def _memkv(mem2d, norm_mem, w_mem_bf):
    depth, d, two_xw = w_mem_bf.shape
    xw = two_xw // 2
    rows = mem2d.shape[0]
    out = lambda dt: jax.ShapeDtypeStruct((depth, rows, xw), dt)
    ospec = pl.BlockSpec((None, rows, xw), lambda l: (l, 0, 0))
    return pl.pallas_call(
        _memkv_body,
        grid=(depth,),
        in_specs=[pl.BlockSpec((rows, d), lambda l: (0, 0)),
                  pl.BlockSpec((None, 1, d), lambda l: (l, 0, 0)),
                  pl.BlockSpec((None, d, two_xw), lambda l: (l, 0, 0))],
        out_specs=[ospec, ospec, ospec, ospec],
        out_shape=[out(F32), out(F32), out(BF16), out(BF16)],
        compiler_params=_cparams(1),
        name="mem_kv",
    )(mem2d, norm_mem.reshape(depth, 1, d), w_mem_bf)


def _proj_attn_body(x_ref, g_ref, w_ref, q_ref, kf_ref, vf_ref, kb_ref, vb_ref, qx_ref):
    mw = q_ref.shape[-1]
    xw = qx_ref.shape[-1]
    xn = _rms(x_ref[...], g_ref[...]).astype(BF16)

    def mm(lo, n):
        return jnp.dot(xn, w_ref[:, lo:lo + n], preferred_element_type=F32)

    n = PROJ_CHUNK
    for c in range(0, mw, n):
        q_ref[:, c:c + n] = (mm(c, n) * (QK_SCALE * LOG2E)).astype(BF16)
        k = mm(mw + c, n)
        kf_ref[:, c:c + n] = k
        kb_ref[:, c:c + n] = k.astype(BF16)
        v = mm(2 * mw + c, n)
        vf_ref[:, c:c + n] = v
        vb_ref[:, c:c + n] = v.astype(BF16)
    qx_ref[...] = mm(3 * mw, xw).astype(BF16)


def _proj_attn(x2d, norm, w_in_bf, layer, tm, mw, xw):
    m, d = x2d.shape
    in_w = w_in_bf.shape[-1]
    row = lambda w: pl.BlockSpec((tm, w), lambda i: (i, 0))
    return pl.pallas_call(
        _proj_attn_body,
        grid=(m // tm,),
        in_specs=[row(d),
                  pl.BlockSpec((None, 1, d), lambda i: (layer, 0, 0)),
                  _resident((None, d, in_w), lambda i: (layer, 0, 0))],
        out_specs=[row(mw), row(mw), row(mw), row(mw), row(mw), row(xw)],
        out_shape=[jax.ShapeDtypeStruct((m, mw), BF16),
                   jax.ShapeDtypeStruct((m, mw), F32),
                   jax.ShapeDtypeStruct((m, mw), F32),
                   jax.ShapeDtypeStruct((m, mw), BF16),
                   jax.ShapeDtypeStruct((m, mw), BF16),
                   jax.ShapeDtypeStruct((m, xw), BF16)],
        compiler_params=_cparams(1),
        name="proj_attn",
    )(x2d, norm, w_in_bf)


def _proj_conv_body(x_ref, g_ref, w_ref, cw_ref, mix_ref, qx_ref, st_ref, *, tiles_per_seq):
    mw = mix_ref.shape[-1]
    xw = qx_ref.shape[-1]
    tm = x_ref.shape[0]
    first = (pl.program_id(0) % tiles_per_seq) == 0
    xn = _rms(x_ref[...], g_ref[...]).astype(BF16)

    def mm(lo, n):
        return jnp.dot(xn, w_ref[:, lo:lo + n], preferred_element_type=F32)

    n = PROJ_CHUNK
    for c in range(0, mw, n):
        z = mm(mw + c, n) * mm(2 * mw + c, n)
        prev = jnp.where(first, 0.0, st_ref[:, c:c + n])
        y = _causal_conv3(z, prev, cw_ref[:, c:c + n])
        st_ref[:, c:c + n] = z[tm - SUBLANES:, :]
        mix_ref[:, c:c + n] = (mm(c, n) * y).astype(BF16)
    qx_ref[...] = mm(3 * mw, xw).astype(BF16)


def _proj_conv(x2d, norm, w_in_bf, conv_w, layer, conv_layer, tm, seq, mw, xw):
    m, d = x2d.shape
    in_w = w_in_bf.shape[-1]
    tps = seq // tm
    row = lambda w: pl.BlockSpec((tm, w), lambda i: (i, 0))
    return pl.pallas_call(
        functools.partial(_proj_conv_body, tiles_per_seq=tps),
        grid=(m // tm,),
        in_specs=[row(d),
                  pl.BlockSpec((None, 1, d), lambda i: (layer, 0, 0)),
                  _resident((None, d, in_w), lambda i: (layer, 0, 0)),
                  pl.BlockSpec((None, conv_w.shape[1], mw), lambda i: (conv_layer, 0, 0))],
        out_specs=[row(mw), row(xw),
                   pl.BlockSpec((None, SUBLANES, mw), lambda i: (i // tps, 0, 0))],
        out_shape=[jax.ShapeDtypeStruct((m, mw), BF16),
                   jax.ShapeDtypeStruct((m, xw), BF16),
                   jax.ShapeDtypeStruct((m // seq, SUBLANES, mw), F32)],
        compiler_params=_cparams(1),
        name="proj_conv",
    )(x2d, norm, w_in_bf, conv_w)


def _attn_body(tab_ref, q_ref, k_ref, vt_ref, bk_ref, lamp_ref, g_ref, o_ref,
               acc, mrun, bias2, buf_a, buf_b, bm_a, bm_b, *, lam_init):
    t = q_ref.shape[0]
    n_buckets = tab_ref.shape[0]
    hp = pl.program_id(1)
    qi = pl.program_id(2)
    nb = t // LANES
    heads = range(ATTN_HEADS)
    hcols = [slice(hh * VAL_DIM, (hh + 1) * VAL_DIM) for hh in heads]
    strips = [slice(c, c + ATTN_STRIP) for c in range(0, t, ATTN_STRIP)]

    @pl.when(qi == 0)
    def _build_bias():
        for hh in heads:
            h = hp * ATTN_HEADS + hh
            far = tab_ref[n_buckets - 1, h]

            def tile(bk):
                out = jnp.zeros(bk.shape, F32)
                for b in range(n_buckets - 1):
                    out = jnp.where(bk == b, (tab_ref[b, h] - far) * LOG2E, out)
                return jnp.where(bk < 0, NEG, out)

            d0 = tile(bk_ref[0])
            d1 = tile(bk_ref[1])
            bias2[hh] = jnp.zeros(bias2.shape[1:], F32)
            bias2[hh, 0, (nb - 1) * LANES:, 0:LANES] = d1
            for r in range(nb):
                for c in range(nb):
                    sub = (hh, 1, slice(r * LANES, (r + 1) * LANES), slice(c * LANES, (c + 1) * LANES))
                    if c < r:
                        bias2[sub] = jnp.full((LANES, LANES), NEG, F32)
                    elif c == r:
                        bias2[sub] = d0
                    elif c == r + 1:
                        bias2[sub] = d1

    lane = lax.broadcasted_iota(jnp.int32, (t, VAL_DIM), 1)
    qz = []
    for hh in heads:
        q = q_ref[:, hcols[hh]]
        zero = jnp.zeros_like(q)
        qz.append((jnp.where(lane < HEAD_DIM, q, zero), jnp.where(lane >= HEAD_DIM, q, zero)))
    ones = jnp.ones((2 * SUBLANES, t), BF16)

    acc[...] = jnp.zeros(acc.shape, F32)
    mrun[...] = jnp.full(mrun.shape, NEG, F32)

    def logits_steps(j, buf, bias_idx=None):
        sbuf, bm = buf

        def step(hh, comp, cs):
            kb = k_ref[pl.ds(pl.multiple_of(j * t, t), t), hcols[hh]]
            s = lax.dot_general(kb, qz[hh][comp][cs, :], NT_DIMS, preferred_element_type=F32)
            if bias_idx is not None:
                s = s + bias2[hh, bias_idx, :, cs]
            sbuf[hh, comp, :, cs] = s
            bm[hh, comp:comp + 1, cs] = jnp.max(s, axis=0, keepdims=True)

        return [functools.partial(step, hh, comp, cs) for hh in heads for comp in range(2) for cs in strips]

    def softmax_steps(j, buf):
        sbuf, bm = buf

        def step(hh, comp, cs):
            vt = jnp.concatenate([vt_ref[hh, j], ones], axis=0)
            m_old = mrun[hh, comp:comp + 1, cs]
            m_new = jnp.maximum(m_old, bm[hh, comp:comp + 1, cs])
            alpha = jnp.exp2(m_old - m_new)
            mrun[hh, comp:comp + 1, cs] = m_new
            p = jnp.exp2(sbuf[hh, comp, :, cs] - m_new).astype(BF16)
            acc[hh, comp, :, cs] = alpha * acc[hh, comp, :, cs] + jnp.dot(vt, p, preferred_element_type=F32)

        return [functools.partial(step, hh, comp, cs) for hh in heads for comp in range(2) for cs in strips]

    def run(*step_lists):
        for group in zip(*step_lists):
            for step in group:
                step()

    buf_a = (buf_a, bm_a)
    buf_b = (buf_b, bm_b)
    PREV, DIAG = 0, 1

    @pl.when(qi == 0)
    def _one_block():
        run(logits_steps(0, buf_a, DIAG))
        run(softmax_steps(0, buf_a))

    @pl.when(qi == 1)
    def _two_blocks():
        run(logits_steps(0, buf_a, PREV))
        run(softmax_steps(0, buf_a), logits_steps(1, buf_b, DIAG))
        run(softmax_steps(1, buf_b))

    @pl.when(qi >= 2)
    def _bias_free_blocks():
        n_pairs = (qi - 2) // 2
        run(logits_steps(0, buf_a))

        def pair(i, carry):
            j = 2 * i
            run(softmax_steps(j, buf_a), logits_steps(j + 1, buf_b))
            run(softmax_steps(j + 1, buf_b), logits_steps(j + 2, buf_a))
            return carry

        lax.fori_loop(0, n_pairs, pair, 0)

    @pl.when((qi >= 2) & (qi % 2 == 0))
    def _even_tail():
        run(softmax_steps(qi - 2, buf_a), logits_steps(qi - 1, buf_b, PREV))
        run(softmax_steps(qi - 1, buf_b), logits_steps(qi, buf_a, DIAG))
        run(softmax_steps(qi, buf_a))

    @pl.when((qi >= 2) & (qi % 2 == 1))
    def _odd_tail():
        run(softmax_steps(qi - 3, buf_a), logits_steps(qi - 2, buf_b))
        run(softmax_steps(qi - 2, buf_b), logits_steps(qi - 1, buf_a, PREV))
        run(softmax_steps(qi - 1, buf_a), logits_steps(qi, buf_b, DIAG))
        run(softmax_steps(qi, buf_b))

    lam = _lambda(lamp_ref[...]) + lam_init
    for hh in heads:
        inv1 = 1.0 / acc[hh, 0, VAL_DIM:VAL_DIM + 1, :]
        inv2 = 1.0 / acc[hh, 1, VAL_DIM:VAL_DIM + 1, :]
        o_t = acc[hh, 0, 0:VAL_DIM, :] * inv1 - lam * (acc[hh, 1, 0:VAL_DIM, :] * inv2)
        o = o_t.T
        y = o * lax.rsqrt(jnp.mean(o * o, axis=-1, keepdims=True) + EPS) * g_ref[...] * (1.0 - lam_init)
        o_ref[:, hcols[hh]] = y.astype(o_ref.dtype)


def _attn_bucket_tiles(n_buckets):
    kk = np.arange(LANES)[:, None]
    qq = np.arange(LANES)[None, :]
    d0 = qq - kk
    t0 = np.where(d0 >= 0, _bucket_np(np.maximum(d0, 0), n_buckets), -1)
    t1 = _bucket_np(LANES + qq - kk, n_buckets)
    return np.stack([t0, t1]).astype(np.int32)


def _attention(q_bf, k_bf, vt_bf, rel_bias, lamp, gain, batch, seq, n_heads, layer):
    t = ATTN_TILE
    nq = seq // t
    assert seq % t == 0 and t >= 2 * LANES and MAX_DISTANCE <= LANES
    m, mw = q_bf.shape
    g = ATTN_HEADS
    assert n_heads % g == 0
    bk = jnp.asarray(_attn_bucket_tiles(rel_bias.shape[0]))
    return pl.pallas_call(
        functools.partial(_attn_body, lam_init=_lambda_init(layer)),
        grid=(batch, n_heads // g, nq),
        in_specs=[pl.BlockSpec(memory_space=pltpu.SMEM),
                  pl.BlockSpec((t, g * VAL_DIM), lambda b, h, i: (b * nq + i, h)),
                  pl.BlockSpec((seq, g * VAL_DIM), lambda b, h, i: (b, h)),
                  pl.BlockSpec((None, g, nq, VAL_DIM, t), lambda b, h, i: (b, h, 0, 0, 0)),
                  pl.BlockSpec((2, LANES, LANES), lambda b, h, i: (0, 0, 0)),
                  pl.BlockSpec((4, HEAD_DIM), lambda b, h, i: (0, 0)),
                  pl.BlockSpec((1, VAL_DIM), lambda b, h, i: (0, 0))],
        out_specs=pl.BlockSpec((t, g * VAL_DIM), lambda b, h, i: (b * nq + i, h)),
        out_shape=jax.ShapeDtypeStruct((m, mw), BF16),
        scratch_shapes=[pltpu.VMEM((g, 2, VAL_DIM + 2 * SUBLANES, t), F32),
                        pltpu.VMEM((g, SUBLANES, t), F32),
                        pltpu.VMEM((g, 2, t, t), F32),
                        pltpu.VMEM((g, 2, t, t), F32),
                        pltpu.VMEM((g, 2, t, t), F32),
                        pltpu.VMEM((g, SUBLANES, t), F32), pltpu.VMEM((g, SUBLANES, t), F32)],
        compiler_params=_cparams(3),
        name="diff_attn",
    )(rel_bias, q_bf, k_bf, vt_bf, bk, lamp, gain)


def _out_body(mix_ref, qx_ref, mk_ref, mv_ref, w_ref, x_ref, o_ref):
    mw = mix_ref.shape[-1]
    xw = qx_ref.shape[-1]
    heads = [slice(c, c + XHEAD_DIM) for c in range(0, xw, XHEAD_DIM)]
    logits = [lax.dot_general(qx_ref[:, hs], mk_ref[:, hs], NT_DIMS, preferred_element_type=F32) * XSCALE
              for hs in heads]
    acc = jnp.dot(mix_ref[...], w_ref[0:mw, :], preferred_element_type=F32)
    xa = []
    for hs, s in zip(heads, logits):
        p = jnp.exp(s - jnp.max(s, axis=-1, keepdims=True))
        inv = 1.0 / jnp.sum(p, axis=-1, keepdims=True)
        xa.append((jnp.dot(p.astype(BF16), mv_ref[:, hs], preferred_element_type=F32) * inv).astype(BF16))
    acc = acc + jnp.dot(jnp.concatenate(xa, axis=-1), w_ref[mw:, :], preferred_element_type=F32)
    o_ref[...] = x_ref[...] + acc


def _out_proj(mix_bf, qx_bf, mk_bf, mv_bf, w_out_bf, x2d, layer, tm, seq):
    m, d = x2d.shape
    mw = mix_bf.shape[-1]
    xw = qx_bf.shape[-1]
    n_mem = mk_bf.shape[1] // (m // seq)
    tps = seq // tm
    row = lambda w: pl.BlockSpec((tm, w), lambda i: (i, 0))
    mem = pl.BlockSpec((None, n_mem, xw), lambda i: (layer, i // tps, 0))
    return pl.pallas_call(
        _out_body,
        grid=(m // tm,),
        in_specs=[row(mw), row(xw), mem, mem,
                  _resident((None, mw + xw, d), lambda i: (layer, 0, 0)),
                  row(d)],
        out_specs=row(d),
        out_shape=jax.ShapeDtypeStruct((m, d), F32),
        compiler_params=_cparams(1),
        name="out_proj",
    )(mix_bf, qx_bf, mk_bf, mv_bf, w_out_bf, x2d)


def _ffn_body(x_ref, g_ref, wu_ref, cw_ref, wd_ref, gf_ref, o_ref, st_ref, act_ref,
              *, tiles_per_seq, final_norm):
    tm = x_ref.shape[0]
    dff = wd_ref.shape[0]
    first = (pl.program_id(0) % tiles_per_seq) == 0
    x = x_ref[...]
    xn = _rms(x, g_ref[...]).astype(BF16)

    def conv_cols(lo, n):
        h = jnp.dot(xn, wu_ref[:, lo:lo + n], preferred_element_type=F32)
        prev = jnp.where(first, 0.0, st_ref[:, lo:lo + n])
        st_ref[:, lo:lo + n] = h[tm - SUBLANES:, :]
        return _causal_conv3(h, prev, cw_ref[:, lo:lo + n])

    n = FFN_CHUNK
    for c in range(0, dff, n):
        act_ref[:, c:c + n] = (_silu(conv_cols(c, n)) * conv_cols(dff + c, n)).astype(BF16)
    y = x + jnp.dot(act_ref[...], wd_ref[...], preferred_element_type=F32)
    if final_norm:
        y = _rms(y, gf_ref[...])
    o_ref[...] = y


def _ffn(x2d, norm, w_up_bf, ffn_conv_w, w_down_bf, norm_final, layer, tm, seq, final_norm):
    m, d = x2d.shape
    dff = w_down_bf.shape[1]
    assert dff % FFN_CHUNK == 0
    tps = seq // tm
    row = pl.BlockSpec((tm, d), lambda i: (i, 0))
    return pl.pallas_call(
        functools.partial(_ffn_body, tiles_per_seq=tps, final_norm=final_norm),
        grid=(m // tm,),
        in_specs=[row,
                  pl.BlockSpec((None, 1, d), lambda i: (layer, 0, 0)),
                  _resident((None, d, 2 * dff), lambda i: (layer, 0, 0)),
                  pl.BlockSpec((None, ffn_conv_w.shape[1], 2 * dff), lambda i: (layer, 0, 0)),
                  _resident((None, dff, d), lambda i: (layer, 0, 0)),
                  pl.BlockSpec((1, d), lambda i: (0, 0))],
        out_specs=[row, pl.BlockSpec((None, SUBLANES, 2 * dff), lambda i: (i // tps, 0, 0))],
        out_shape=[jax.ShapeDtypeStruct((m, d), F32),
                   jax.ShapeDtypeStruct((m // seq, SUBLANES, 2 * dff), F32)],
        scratch_shapes=[pltpu.VMEM((tm, dff), BF16)],
        compiler_params=_cparams(1),
        name="conv_ffn",
    )(x2d, norm, w_up_bf, ffn_conv_w, w_down_bf, norm_final)


def _proj_plain_body(x_ref, g_ref, w_ref, u_ref):
    xn = _rms(x_ref[...], g_ref[...]).astype(BF16)
    u_ref[...] = jnp.dot(xn, w_ref[...], preferred_element_type=F32)


def _proj_plain(x2d, norm, w_in_bf, layer):
    m, d = x2d.shape
    in_w = w_in_bf.shape[-1]
    n = PROJ_CHUNK
    return pl.pallas_call(
        _proj_plain_body,
        grid=(in_w // n,),
        in_specs=[pl.BlockSpec((m, d), lambda c: (0, 0)),
                  pl.BlockSpec((None, 1, d), lambda c: (layer, 0, 0)),
                  pl.BlockSpec((None, d, n), lambda c: (layer, 0, c))],
        out_specs=pl.BlockSpec((m, n), lambda c: (0, c)),
        out_shape=jax.ShapeDtypeStruct((m, in_w), F32),
        compiler_params=_cparams(1),
        name="proj_sample",
    )(x2d, norm, w_in_bf)


def _paged_attn_body(pt_ref, q_ref, kn_ref, vn_ref, bks_ref, tab_ref, lamp_ref, g_ref, *rest,
                     n_pages, lam_init):
    k_refs = rest[:n_pages]
    v_refs = rest[n_pages:2 * n_pages]
    o_ref, s_ref, bias_ref = rest[2 * n_pages:]
    rows = k_refs[0].shape[0]
    n_heads = kn_ref.shape[0]
    past = n_pages * rows
    n_buckets = tab_ref.shape[1]

    @pl.when(pl.program_id(0) == 0)
    def _build_bias():
        bk = bks_ref[...]
        out = jnp.zeros(bk.shape, F32)
        for b in range(n_buckets):
            out = jnp.where(bk == b, tab_ref[:, b:b + 1], out)
        bias_ref[...] = jnp.where(bk < 0, NEG, out)

    q = q_ref[...]
    for r in range(n_pages):
        s_ref[:, r * rows:(r + 1) * rows] = lax.dot_general(
            q, k_refs[r][...], NT_DIMS, preferred_element_type=F32)
    pad = jnp.zeros((LANES - n_heads, VAL_DIM), F32)
    s_ref[:, past:] = lax.dot_general(q, jnp.concatenate([kn_ref[...], pad], axis=0), NT_DIMS,
                                      preferred_element_type=F32)

    s = s_ref[...] + bias_ref[...]
    p = jnp.exp(s - jnp.max(s, axis=-1, keepdims=True))
    pn = p * (1.0 / jnp.sum(p, axis=-1, keepdims=True))
    pv = jnp.dot(pn[:, past:], jnp.concatenate([vn_ref[...], pad], axis=0), preferred_element_type=F32)
    for r in range(n_pages):
        pv = pv + jnp.dot(pn[:, r * rows:(r + 1) * rows], v_refs[r][...], preferred_element_type=F32)
    lam = _lambda(lamp_ref[...]) + lam_init
    o = pv[0:n_heads, :] - lam * pv[n_heads:, :]
    o_ref[...] = o * lax.rsqrt(jnp.mean(o * o, axis=-1, keepdims=True) + EPS) * g_ref[...] * (1.0 - lam_init)


def _paged_attention(q, k_new, v_new, cache_k, cache_v, page_table, rel_bias, lamp, gain, layer, attn_layer):
    db, width = q.shape
    n_layers, n_phys, page, n_heads, vd = cache_k.shape
    n_pages = page_table.shape[1]
    n_buckets = rel_bias.shape[0]
    rows = page * n_heads
    past = n_pages * rows
    assert vd == VAL_DIM == LANES and n_heads <= LANES
    ck = cache_k.reshape(n_layers, n_phys, rows, vd)
    cv = cache_v.reshape(n_layers, n_phys, rows, vd)
    q3 = q.reshape(db, 1, n_heads, vd) * QK_SCALE
    comp = (np.arange(vd) // HEAD_DIM)[None, None, None, :] == np.arange(2)[None, :, None, None]
    qrows = jnp.where(comp, q3, 0.0).reshape(db, 2 * n_heads, vd)
    col = np.arange(past + LANES)
    key = np.minimum(col // n_heads, past // n_heads)
    valid = (col < past + n_heads)[None, :] & ((col % n_heads)[None, :]
                                               == (np.arange(2 * n_heads) % n_heads)[:, None])
    bucket = _bucket_np(past // n_heads - key, n_buckets)
    bks = jnp.asarray(np.where(valid, bucket[None, :], -1).astype(np.int32))
    tab = jnp.tile(rel_bias.T, (2, 1))
    head_rows = lambda a: a.reshape(db, n_heads, vd)
    hspec = pl.BlockSpec((None, n_heads, vd), lambda b, pt: (b, 0, 0))

    def page_spec(r):
        return pl.BlockSpec((None, None, rows, vd), lambda b, pt: (attn_layer, pt[b, r], 0, 0))

    const = lambda shape: pl.BlockSpec(shape, lambda b, pt: (0,) * len(shape))
    grid_spec = pltpu.PrefetchScalarGridSpec(
        num_scalar_prefetch=1,
        grid=(db,),
        in_specs=[pl.BlockSpec((None, 2 * n_heads, vd), lambda b, pt: (b, 0, 0)), hspec, hspec,
                  const(bks.shape), const(tab.shape), const((4, HEAD_DIM)), const((1, VAL_DIM))]
                 + [page_spec(r) for r in range(n_pages)] * 2,
        out_specs=hspec,
        scratch_shapes=[pltpu.VMEM(bks.shape, F32), pltpu.VMEM(bks.shape, F32)],
    )
    out = pl.pallas_call(
        functools.partial(_paged_attn_body, n_pages=n_pages, lam_init=_lambda_init(layer)),
        grid_spec=grid_spec,
        out_shape=jax.ShapeDtypeStruct((db, n_heads, vd), F32),
        compiler_params=_cparams(1),
        name="paged_attn",
    )(page_table, qrows, head_rows(k_new), head_rows(v_new), bks, tab, lamp, gain,
      *([ck] * n_pages), *([cv] * n_pages))
    return out.reshape(db, width)


def _mem_sample_body(qx_ref, mk_ref, mv_ref, o_ref, *, n_xheads):
    g, rows, _ = mk_ref.shape
    col = lax.broadcasted_iota(jnp.int32, (SUBLANES, rows), 1)
    row = lax.broadcasted_iota(jnp.int32, (SUBLANES, rows), 0)
    own = col % n_xheads == row
    for b in range(g):
        s = lax.dot_general(qx_ref[b], mk_ref[b], NT_DIMS, preferred_element_type=F32) * XSCALE
        s = jnp.where(own, s, NEG)
        p = jnp.exp(s - jnp.max(s, axis=-1, keepdims=True))
        pn = p * (1.0 / jnp.sum(p, axis=-1, keepdims=True))
        o_ref[b] = jnp.dot(pn, mv_ref[b], preferred_element_type=F32)


def _mem_sample(qx, cache_mem_k, cache_mem_v, layer):
    db, xw = qx.shape
    depth, _, n_mem, n_xheads, xd = cache_mem_k.shape
    assert xd == XHEAD_DIM == LANES and n_xheads <= SUBLANES
    g = SUBLANES
    rows = n_mem * n_xheads
    mk = cache_mem_k.reshape(depth, db, rows, xd)
    mv = cache_mem_v.reshape(depth, db, rows, xd)
    qx8 = jnp.pad(qx.reshape(db, n_xheads, xd), ((0, 0), (0, SUBLANES - n_xheads), (0, 0)))
    mem = pl.BlockSpec((None, g, rows, xd), lambda i: (layer, i, 0, 0))
    vec = pl.BlockSpec((g, SUBLANES, xd), lambda i: (i, 0, 0))
    out = pl.pallas_call(
        functools.partial(_mem_sample_body, n_xheads=n_xheads),
        grid=(db // g,),
        in_specs=[vec, mem, mem],
        out_specs=vec,
        out_shape=jax.ShapeDtypeStruct((db, SUBLANES, xd), F32),
        compiler_params=_cparams(1),
        name="mem_attn_sample",
    )(qx8, mk, mv)
    return out[:, :n_xheads, :].reshape(db, xw)


def _out_sample_attn_body(mix_ref, xa_ref, w_ref, x_ref, y_ref):
    mw = mix_ref.shape[-1]
    acc = jnp.dot(mix_ref[...].astype(BF16), w_ref[0:mw, :], preferred_element_type=F32)
    acc = acc + jnp.dot(xa_ref[...].astype(BF16), w_ref[mw:, :], preferred_element_type=F32)
    y_ref[...] = x_ref[...] + acc


def _out_sample_conv_body(u_ref, st_ref, cw_ref, xa_ref, w_ref, x_ref, y_ref, z_ref):
    mw = z_ref.shape[-1]
    z = u_ref[:, mw:2 * mw] * u_ref[:, 2 * mw:3 * mw]
    conv = st_ref[:, 0:mw] * cw_ref[0:1, :] + st_ref[:, mw:] * cw_ref[1:2, :] + z * cw_ref[2:3, :]
    mix = u_ref[:, 0:mw] * conv
    z_ref[...] = z
    acc = jnp.dot(mix.astype(BF16), w_ref[0:mw, :], preferred_element_type=F32)
    acc = acc + jnp.dot(xa_ref[...].astype(BF16), w_ref[mw:, :], preferred_element_type=F32)
    y_ref[...] = x_ref[...] + acc


def _full(a):
    return pl.BlockSpec(a.shape, lambda i: (0,) * a.ndim)


def _out_sample_attn(mix, xa, w_out_bf, x2d, layer):
    w_spec = pl.BlockSpec((None,) + w_out_bf.shape[1:], lambda i: (layer, 0, 0))
    return pl.pallas_call(
        _out_sample_attn_body,
        grid=(1,),
        in_specs=[_full(mix), _full(xa), w_spec, _full(x2d)],
        out_specs=_full(x2d),
        out_shape=jax.ShapeDtypeStruct(x2d.shape, F32),
        compiler_params=_cparams(1),
        name="out_sample_attn",
    )(mix, xa, w_out_bf, x2d)


def _out_sample_conv(u, state2d, conv_w, xa, w_out_bf, x2d, layer, conv_layer, mw):
    db = u.shape[0]
    w_spec = pl.BlockSpec((None,) + w_out_bf.shape[1:], lambda i: (layer, 0, 0))
    cw_spec = pl.BlockSpec((None,) + conv_w.shape[1:], lambda i: (conv_layer, 0, 0))
    z_shape = jax.ShapeDtypeStruct((db, mw), F32)
    return pl.pallas_call(
        _out_sample_conv_body,
        grid=(1,),
        in_specs=[_full(u), _full(state2d), cw_spec, _full(xa), w_spec, _full(x2d)],
        out_specs=[_full(x2d), _full(z_shape)],
        out_shape=[jax.ShapeDtypeStruct(x2d.shape, F32), z_shape],
        compiler_params=_cparams(1),
        name="out_sample_conv",
    )(u, state2d, conv_w, xa, w_out_bf, x2d)


def _ffn_sample_body(x_ref, g_ref, wg_ref, wu_ref, cg_ref, cu_ref, s0g_ref, s0u_ref, s1g_ref, s1u_ref,
                     wd_ref, gf_ref, y_ref, hg_ref, hu_ref, xn_ref, *, final_norm):
    c = pl.program_id(0)

    @pl.when(c == 0)
    def _init():
        xn_ref[...] = _rms(x_ref[...], g_ref[...]).astype(BF16)
        y_ref[...] = x_ref[...]

    def step(w_ref, cw_ref, s0_ref, s1_ref, h_ref):
        h = jnp.dot(xn_ref[...], w_ref[...], preferred_element_type=F32)
        h_ref[...] = h
        return s0_ref[...] * cw_ref[0:1, :] + s1_ref[...] * cw_ref[1:2, :] + h * cw_ref[2:3, :]

    gate = step(wg_ref, cg_ref, s0g_ref, s1g_ref, hg_ref)
    up = step(wu_ref, cu_ref, s0u_ref, s1u_ref, hu_ref)
    act = (_silu(gate) * up).astype(BF16)
    y_ref[...] += jnp.dot(act, wd_ref[...], preferred_element_type=F32)

    if final_norm:
        @pl.when(c == pl.num_programs(0) - 1)
        def _finish():
            y_ref[...] = _rms(y_ref[...], gf_ref[...])


def _ffn_sample(x2d, norm, w_up_bf, ffn_conv_w, w_down_bf, state2d, norm_final, layer, final_norm):
    m, d = x2d.shape
    dff = w_down_bf.shape[1]
    n = FFN_CHUNK
    nch = dff // n
    kw = ffn_conv_w.shape[1]
    xs = pl.BlockSpec((m, d), lambda c: (0, 0))
    col = lambda off: pl.BlockSpec((m, n), lambda c: (0, off + c))
    return pl.pallas_call(
        functools.partial(_ffn_sample_body, final_norm=final_norm),
        grid=(nch,),
        in_specs=[xs,
                  pl.BlockSpec((None, 1, d), lambda c: (layer, 0, 0)),
                  pl.BlockSpec((None, d, n), lambda c: (layer, 0, c)),
                  pl.BlockSpec((None, d, n), lambda c: (layer, 0, nch + c)),
                  pl.BlockSpec((None, kw, n), lambda c: (layer, 0, c)),
                  pl.BlockSpec((None, kw, n), lambda c: (layer, 0, nch + c)),
                  col(0), col(nch), col(2 * nch), col(3 * nch),
                  pl.BlockSpec((None, n, d), lambda c: (layer, c, 0)),
                  pl.BlockSpec((1, d), lambda c: (0, 0))],
        out_specs=[xs, col(0), col(0)],
        out_shape=[jax.ShapeDtypeStruct((m, d), F32),
                   jax.ShapeDtypeStruct((m, dff), F32),
                   jax.ShapeDtypeStruct((m, dff), F32)],
        scratch_shapes=[pltpu.VMEM((m, d), BF16)],
        compiler_params=_cparams(1),
        name="conv_ffn_sample",
    )(x2d, norm, w_up_bf, w_up_bf, ffn_conv_w, ffn_conv_w, state2d, state2d, state2d, state2d,
      w_down_bf, norm_final)


def kernel(x_prompt, x_sample, mem_prompt, cache_k, cache_v, page_table, cache_mem_k, cache_mem_v,
           state_conv, state_ffn, rel_bias, w_in, w_out, norm_mix, norm_mem, w_mem_kv,
           lambda_q1, lambda_k1, lambda_q2, lambda_k2, subln_gain, conv_w,
           norm_ffn, w_up, ffn_conv_w, w_down, norm_final):
    batch, seq, d = x_prompt.shape
    db = x_sample.shape[0]
    depth = w_in.shape[0]
    n_mem = mem_prompt.shape[1]
    n_heads = cache_k.shape[3]
    mw = n_heads * VAL_DIM
    xw = w_mem_kv.shape[-1] // 2
    n_xheads = xw // XHEAD_DIM
    dff = w_down.shape[1]
    tm = min(ROW_TILE, seq)
    assert x_sample.shape[1] == 1 and seq % tm == 0 and w_in.shape[-1] == 3 * mw + xw

    w_in_bf = w_in.astype(BF16)
    w_out_bf = w_out.astype(BF16)
    w_up_bf = w_up.astype(BF16)
    w_down_bf = w_down.astype(BF16)
    w_mem_bf = w_mem_kv.astype(BF16)
    norm_mix3 = norm_mix.reshape(depth, 1, d)
    norm_ffn3 = norm_ffn.reshape(depth, 1, d)
    norm_final2 = norm_final.reshape(1, d)
    nt = seq // ATTN_TILE

    xp = x_prompt.reshape(batch * seq, d)
    xs = x_sample.reshape(db, d)

    mk_f, mv_f, mk_bf, mv_bf = _memkv(mem_prompt.reshape(batch * n_mem, d), norm_mem, w_mem_bf)

    k_rows_p, v_rows_p, k_rows_s, v_rows_s = [], [], [], []
    conv_st_p, conv_st_s, ffn_st_p, ffn_st_s = [], [], [], []

    for i in range(depth):
        li = i // 2
        last = i == depth - 1
        if i % 2 == 0:
            lamp = jnp.stack([lambda_q1[li], lambda_k1[li], lambda_q2[li], lambda_k2[li]])
            gain = subln_gain[li].reshape(1, VAL_DIM)
            q_bf, k_f, v_f, k_bf, v_bf, qx_bf = _proj_attn(xp, norm_mix3, w_in_bf, i, tm, mw, xw)
            vt_bf = v_bf.reshape(batch, nt, ATTN_TILE, n_heads, VAL_DIM).transpose(0, 3, 1, 4, 2)
            mix_bf = _attention(q_bf, k_bf, vt_bf, rel_bias, lamp, gain, batch, seq, n_heads, i)
            k_rows_p.append(k_f.reshape(batch, seq, n_heads, VAL_DIM))
            v_rows_p.append(v_f.reshape(batch, seq, n_heads, VAL_DIM))
            us = _proj_plain(xs, norm_mix3, w_in_bf, i)
            qs, ks, vs = us[:, :mw], us[:, mw:2 * mw], us[:, 2 * mw:3 * mw]
            mix_s = _paged_attention(qs, ks, vs, cache_k, cache_v, page_table, rel_bias, lamp, gain, i, li)
            xa_s = _mem_sample(us[:, 3 * mw:], cache_mem_k, cache_mem_v, i)
            xs = _out_sample_attn(mix_s, xa_s, w_out_bf, xs, i)
            k_rows_s.append(ks.reshape(db, 1, n_heads, VAL_DIM))
            v_rows_s.append(vs.reshape(db, 1, n_heads, VAL_DIM))
        else:
            mix_bf, qx_bf, cst = _proj_conv(xp, norm_mix3, w_in_bf, conv_w, i, li, tm, seq, mw, xw)
            conv_st_p.append(cst[:, SUBLANES - (conv_w.shape[1] - 1):, :])
            us = _proj_plain(xs, norm_mix3, w_in_bf, i)
            xa_s = _mem_sample(us[:, 3 * mw:], cache_mem_k, cache_mem_v, i)
            st = state_conv[li]
            xs, z_s = _out_sample_conv(us, st.reshape(db, -1), conv_w, xa_s, w_out_bf, xs, i, li, mw)
            conv_st_s.append(jnp.stack([st[:, 1, :], z_s], axis=1))

        xp = _out_proj(mix_bf, qx_bf, mk_bf, mv_bf, w_out_bf, xp, i, tm, seq)
        xp, fst = _ffn(xp, norm_ffn3, w_up_bf, ffn_conv_w, w_down_bf, norm_final2, i, tm, seq, last)
        ffn_st_p.append(fst[:, SUBLANES - (ffn_conv_w.shape[1] - 1):, :])

        fs = state_ffn[i]
        xs, hg, hu = _ffn_sample(xs, norm_ffn3, w_up_bf, ffn_conv_w, w_down_bf, fs.reshape(db, -1),
                                 norm_final2, i, last)
        ffn_st_s.append(jnp.stack([fs[:, 1, :], jnp.concatenate([hg, hu], axis=-1)], axis=1))

    mem_shape = (depth, batch, n_mem, n_xheads, XHEAD_DIM)
    return (xp.reshape(batch, seq, d), xs.reshape(db, 1, d),
            jnp.stack(k_rows_p), jnp.stack(v_rows_p), jnp.stack(k_rows_s), jnp.stack(v_rows_s),
            mk_f.reshape(mem_shape), mv_f.reshape(mem_shape),
            jnp.stack(conv_st_p), jnp.stack(conv_st_s), jnp.stack(ffn_st_p), jnp.stack(ffn_st_s))
```

```python
import functools
import math

import numpy as np
import jax
import jax.numpy as jnp
from jax import lax
from jax.experimental import pallas as pl
from jax.experimental.pallas import tpu as pltpu

F32 = jnp.float32
BF16 = jnp.bfloat16

HEAD_DIM = 64
VAL_DIM = 2 * HEAD_DIM
XHEAD_DIM = 128
MAX_DISTANCE = 128
EPS = 1e-6
QK_SCALE = HEAD_DIM ** -0.5
XSCALE = XHEAD_DIM ** -0.5
LOG2E = math.log2(math.e)
NEG = -1e30

LANES = 128
SUBLANES = 8
VMEM_LIMIT_MB = 56

ATTN_TILE = 512
ATTN_STRIP = 256
ATTN_HEADS = 2
ROW_TILE = 512
FFN_CHUNK = 256
FFN_TILE = 512
FFN_SPLIT = 1
PROJ_CHUNK = 512

NT_DIMS = (((1,), (1,)), ((), ()))


def _cparams(n_axes):
    return pltpu.CompilerParams(dimension_semantics=("arbitrary",) * n_axes,
                                vmem_limit_bytes=VMEM_LIMIT_MB * 1024 * 1024)


def _resident(block_shape, index_map):
    return pl.BlockSpec(block_shape, index_map, pipeline_mode=pl.Buffered(1))


def _rms(x, g):
    return x * lax.rsqrt(jnp.mean(x * x, axis=-1, keepdims=True) + EPS) * g


def _lambda_init(layer_idx):
    return 0.8 - 0.6 * math.exp(-0.3 * layer_idx)


def _lambda(lamp):
    a = jnp.sum(lamp[0:1] * lamp[1:2], axis=-1, keepdims=True)
    b = jnp.sum(lamp[2:3] * lamp[3:4], axis=-1, keepdims=True)
    return jnp.exp(a) - jnp.exp(b)


def _bucket_np(n, n_buckets):
    max_exact = n_buckets // 2
    nf = np.maximum(n, 1).astype(np.float32)
    large = max_exact + (np.log(nf / np.float32(max_exact)) / np.float32(math.log(MAX_DISTANCE / max_exact))
                         * np.float32(n_buckets - max_exact)).astype(np.int32)
    large = np.minimum(large, n_buckets - 1)
    return np.where(n < max_exact, n, large).astype(np.int32)


def _causal_conv3(z, prev, w):
    def taps(a):
        return pltpu.roll(a, 2, 0) * w[0:1, :] + pltpu.roll(a, 1, 0) * w[1:2, :] + a * w[2:3, :]

    head = taps(jnp.concatenate([prev, z[0:SUBLANES, :]], axis=0))[SUBLANES:, :]
    return jnp.concatenate([head, taps(z)[SUBLANES:, :]], axis=0)


def _silu(g):
    return g * (1.0 / (1.0 + jnp.exp(-g)))


def _memkv_body(mem_ref, g_ref, w_ref, kf_ref, vf_ref, kb_ref, vb_ref):
    xw = kf_ref.shape[-1]
    xn = _rms(mem_ref[...], g_ref[...]).astype(BF16)
    k = jnp.dot(xn, w_ref[:, :xw], preferred_element_type=F32)
    v = jnp.dot(xn, w_ref[:, xw:], preferred_element_type=F32)
    kf_ref[...] = k
    vf_ref[...] = v
    kb_ref[...] = k.astype(BF16)
    vb_ref[...] = v.astype(BF16)


def _memkv(mem2d, norm_mem, w_mem_bf):
    depth, d, two_xw = w_mem_bf.shape
    xw = two_xw // 2
    rows = mem2d.shape[0]
    out = lambda dt: jax.ShapeDtypeStruct((depth, rows, xw), dt)
    ospec = pl.BlockSpec((None, rows, xw), lambda l: (l, 0, 0))
    return pl.pallas_call(
        _memkv_body,
        grid=(depth,),
        in_specs=[pl.BlockSpec((rows, d), lambda l: (0, 0)),
                  pl.BlockSpec((None, 1, d), lambda l: (l, 0, 0)),
                  pl.BlockSpec((None, d, two_xw), lambda l: (l, 0, 0))],
        out_specs=[ospec, ospec, ospec, ospec],
        out_shape=[out(F32), out(F32), out(BF16), out(BF16)],
        compiler_params=_cparams(1),
        name="mem_kv",
    )(mem2d, norm_mem.reshape(depth, 1, d), w_mem_bf)


def _proj_attn_body(x_ref, g_ref, w_ref, *refs, first):
    if first:
        q_ref, kf_ref, vf_ref, kb_ref, vt_ref, qx_ref = refs
        for stack in (kf_ref, vf_ref):
            for other in range(1, stack.shape[0]):
                stack[other] = jnp.zeros(stack.shape[1:], F32)
        kf_ref, vf_ref = kf_ref.at[0], vf_ref.at[0]
    else:
        _, _, q_ref, kf_ref, vf_ref, kb_ref, vt_ref, qx_ref = refs
    mw = q_ref.shape[-1]
    xw = qx_ref.shape[-1]
    xn = _rms(x_ref[...], g_ref[...]).astype(BF16)

    def mm(lo, n):
        return jnp.dot(xn, w_ref[:, lo:lo + n], preferred_element_type=F32)

    n = PROJ_CHUNK
    for c in range(0, mw, n):
        q_ref[:, c:c + n] = (mm(c, n) * (QK_SCALE * LOG2E)).astype(BF16)
        k = mm(mw + c, n)
        kf_ref[:, c:c + n] = k
        kb_ref[:, c:c + n] = k.astype(BF16)
        v = mm(2 * mw + c, n)
        vf_ref[:, c:c + n] = v
        for hc in range(0, n, VAL_DIM):
            vt_ref[(c + hc) // VAL_DIM] = v[:, hc:hc + VAL_DIM].T.astype(BF16)
    qx_ref[...] = mm(3 * mw, xw).astype(BF16)


def _proj_attn(x2d, norm, w_in_bf, layer, tm, seq, mw, xw, n_slabs, slab, stacks):
    m, d = x2d.shape
    in_w = w_in_bf.shape[-1]
    tps = seq // tm
    n_heads = mw // VAL_DIM
    first = slab == 0
    row = lambda w: pl.BlockSpec((tm, w), lambda i: (i, 0))
    if first:
        stack_spec = pl.BlockSpec((n_slabs, tm, mw), lambda i: (0, i, 0))
        extra_specs, extra_args, aliases = [], (), {}
    else:
        stack_spec = pl.BlockSpec((None, tm, mw), lambda i: (slab, i, 0))
        extra_specs = [pl.BlockSpec(memory_space=pl.ANY)] * 2
        extra_args, aliases = tuple(stacks), {3: 1, 4: 2}
    stack_shape = jax.ShapeDtypeStruct((n_slabs, m, mw), F32)
    return pl.pallas_call(
        functools.partial(_proj_attn_body, first=first),
        grid=(m // tm,),
        in_specs=[row(d),
                  pl.BlockSpec((None, 1, d), lambda i: (layer, 0, 0)),
                  _resident((None, d, in_w), lambda i: (layer, 0, 0))] + extra_specs,
        out_specs=[row(mw), stack_spec, stack_spec, row(mw),
                   pl.BlockSpec((None, n_heads, None, VAL_DIM, tm), lambda i: (i // tps, 0, i % tps, 0, 0)),
                   row(xw)],
        out_shape=[jax.ShapeDtypeStruct((m, mw), BF16), stack_shape, stack_shape,
                   jax.ShapeDtypeStruct((m, mw), BF16),
                   jax.ShapeDtypeStruct((m // seq, n_heads, tps, VAL_DIM, tm), BF16),
                   jax.ShapeDtypeStruct((m, xw), BF16)],
        input_output_aliases=aliases,
        compiler_params=_cparams(1),
        name="proj_attn",
    )(x2d, norm, w_in_bf, *extra_args)


def _proj_conv_body(x_ref, g_ref, w_ref, cw_ref, mix_ref, qx_ref, st_ref, *, tiles_per_seq):
    mw = mix_ref.shape[-1]
    xw = qx_ref.shape[-1]
    tm = x_ref.shape[0]
    first = (pl.program_id(0) % tiles_per_seq) == 0
    xn = _rms(x_ref[...], g_ref[...]).astype(BF16)

    def mm(lo, n):
        return jnp.dot(xn, w_ref[:, lo:lo + n], preferred_element_type=F32)

    n = PROJ_CHUNK
    for c in range(0, mw, n):
        z = mm(mw + c, n) * mm(2 * mw + c, n)
        prev = jnp.where(first, 0.0, st_ref[:, c:c + n])
        y = _causal_conv3(z, prev, cw_ref[:, c:c + n])
        st_ref[:, c:c + n] = z[tm - SUBLANES:, :]
        mix_ref[:, c:c + n] = (mm(c, n) * y).astype(BF16)
    qx_ref[...] = mm(3 * mw, xw).astype(BF16)


def _proj_conv(x2d, norm, w_in_bf, conv_w, layer, conv_layer, tm, seq, mw, xw):
    m, d = x2d.shape
    in_w = w_in_bf.shape[-1]
    tps = seq // tm
    row = lambda w: pl.BlockSpec((tm, w), lambda i: (i, 0))
    return pl.pallas_call(
        functools.partial(_proj_conv_body, tiles_per_seq=tps),
        grid=(m // tm,),
        in_specs=[row(d),
                  pl.BlockSpec((None, 1, d), lambda i: (layer, 0, 0)),
                  _resident((None, d, in_w), lambda i: (layer, 0, 0)),
                  pl.BlockSpec((None, conv_w.shape[1], mw), lambda i: (conv_layer, 0, 0))],
        out_specs=[row(mw), row(xw),
                   pl.BlockSpec((None, SUBLANES, mw), lambda i: (i // tps, 0, 0))],
        out_shape=[jax.ShapeDtypeStruct((m, mw), BF16),
                   jax.ShapeDtypeStruct((m, xw), BF16),
                   jax.ShapeDtypeStruct((m // seq, SUBLANES, mw), F32)],
        compiler_params=_cparams(1),
        name="proj_conv",
    )(x2d, norm, w_in_bf, conv_w)


def _attn_body(tab_ref, q_ref, k_ref, vt_ref, bk_ref, lamp_ref, g_ref, o_ref,
               acc, mrun, bias2, buf_a, buf_b, bm_a, bm_b, *, lam_init):
    t = q_ref.shape[0]
    n_buckets = tab_ref.shape[0]
    hp = pl.program_id(1)
    qi = pl.program_id(2)
    nb = t // LANES
    heads = range(ATTN_HEADS)
    hcols = [slice(hh * VAL_DIM, (hh + 1) * VAL_DIM) for hh in heads]
    strips = [slice(c, c + ATTN_STRIP) for c in range(0, t, ATTN_STRIP)]

    @pl.when(qi == 0)
    def _build_bias():
        for hh in heads:
            h = hp * ATTN_HEADS + hh
            far = tab_ref[n_buckets - 1, h]

            def tile(bk):
                out = jnp.zeros(bk.shape, F32)
                for b in range(n_buckets - 1):
                    out = jnp.where(bk == b, (tab_ref[b, h] - far) * LOG2E, out)
                return jnp.where(bk < 0, NEG, out)

            d0 = tile(bk_ref[0])
            d1 = tile(bk_ref[1])
            bias2[hh] = jnp.zeros(bias2.shape[1:], F32)
            bias2[hh, 0, (nb - 1) * LANES:, 0:LANES] = d1
            for r in range(nb):
                for c in range(nb):
                    sub = (hh, 1, slice(r * LANES, (r + 1) * LANES), slice(c * LANES, (c + 1) * LANES))
                    if c < r:
                        bias2[sub] = jnp.full((LANES, LANES), NEG, F32)
                    elif c == r:
                        bias2[sub] = d0
                    elif c == r + 1:
                        bias2[sub] = d1

    lane = lax.broadcasted_iota(jnp.int32, (t, VAL_DIM), 1)
    qz = []
    for hh in heads:
        q = q_ref[:, hcols[hh]]
        zero = jnp.zeros_like(q)
        qz.append((jnp.where(lane < HEAD_DIM, q, zero), jnp.where(lane >= HEAD_DIM, q, zero)))
    ones = jnp.ones((2 * SUBLANES, t), BF16)

    acc[...] = jnp.zeros(acc.shape, F32)
    mrun[...] = jnp.full(mrun.shape, NEG, F32)

    PREV, DIAG = 0, 1

    def n_keys(cs, diag):
        return cs.stop if diag else t

    def logits_steps(j, buf, bias_idx=None):
        sbuf, bm = buf

        def step(hh, comp, cs):
            keys = n_keys(cs, bias_idx == DIAG)
            kb = k_ref[pl.ds(pl.multiple_of(j * t, t), keys), hcols[hh]]
            s = lax.dot_general(kb, qz[hh][comp][cs, :], NT_DIMS, preferred_element_type=F32)
            if bias_idx is not None:
                s = s + bias2[hh, bias_idx, 0:keys, cs]
            sbuf[hh, comp, 0:keys, cs] = s
            bm[hh, comp:comp + 1, cs] = jnp.max(s, axis=0, keepdims=True)

        return [functools.partial(step, hh, comp, cs) for hh in heads for comp in range(2) for cs in strips]

    def softmax_steps(j, buf, diag=False):
        sbuf, bm = buf

        def step(hh, comp, cs):
            keys = n_keys(cs, diag)
            vt = jnp.concatenate([vt_ref[hh, j], ones], axis=0)
            m_old = mrun[hh, comp:comp + 1, cs]
            m_new = jnp.maximum(m_old, bm[hh, comp:comp + 1, cs])
            alpha = jnp.exp2(m_old - m_new)
            mrun[hh, comp:comp + 1, cs] = m_new
            p = jnp.exp2(sbuf[hh, comp, 0:keys, cs] - m_new).astype(BF16)
            acc[hh, comp, :, cs] = (alpha * acc[hh, comp, :, cs]
                                    + jnp.dot(vt[:, 0:keys], p, preferred_element_type=F32))

        return [functools.partial(step, hh, comp, cs) for hh in heads for comp in range(2) for cs in strips]

    def run(*step_lists):
        for group in zip(*step_lists):
            for step in group:
                step()

    buf_a = (buf_a, bm_a)
    buf_b = (buf_b, bm_b)

    @pl.when(qi == 0)
    def _one_block():
        run(logits_steps(0, buf_a, DIAG))
        run(softmax_steps(0, buf_a, diag=True))

    @pl.when(qi == 1)
    def _two_blocks():
        run(logits_steps(0, buf_a, PREV))
        run(softmax_steps(0, buf_a), logits_steps(1, buf_b, DIAG))
        run(softmax_steps(1, buf_b, diag=True))

    @pl.when(qi >= 2)
    def _bias_free_blocks():
        n_pairs = (qi - 2) // 2
        run(logits_steps(0, buf_a))

        def pair(i, carry):
            j = 2 * i
            run(softmax_steps(j, buf_a), logits_steps(j + 1, buf_b))
            run(softmax_steps(j + 1, buf_b), logits_steps(j + 2, buf_a))
            return carry

        lax.fori_loop(0, n_pairs, pair, 0)

    @pl.when((qi >= 2) & (qi % 2 == 0))
    def _even_tail():
        run(softmax_steps(qi - 2, buf_a), logits_steps(qi - 1, buf_b, PREV))
        run(softmax_steps(qi - 1, buf_b), logits_steps(qi, buf_a, DIAG))
        run(softmax_steps(qi, buf_a, diag=True))

    @pl.when((qi >= 2) & (qi % 2 == 1))
    def _odd_tail():
        run(softmax_steps(qi - 3, buf_a), logits_steps(qi - 2, buf_b))
        run(softmax_steps(qi - 2, buf_b), logits_steps(qi - 1, buf_a, PREV))
        run(softmax_steps(qi - 1, buf_a), logits_steps(qi, buf_b, DIAG))
        run(softmax_steps(qi, buf_b, diag=True))

    lam = _lambda(lamp_ref[...]) + lam_init
    for hh in heads:
        inv1 = 1.0 / acc[hh, 0, VAL_DIM:VAL_DIM + 1, :]
        inv2 = 1.0 / acc[hh, 1, VAL_DIM:VAL_DIM + 1, :]
        o_t = acc[hh, 0, 0:VAL_DIM, :] * inv1 - lam * (acc[hh, 1, 0:VAL_DIM, :] * inv2)
        o = o_t.T
        y = o * lax.rsqrt(jnp.mean(o * o, axis=-1, keepdims=True) + EPS) * g_ref[...] * (1.0 - lam_init)
        o_ref[:, hcols[hh]] = y.astype(o_ref.dtype)


def _attn_bucket_tiles(n_buckets):
    kk = np.arange(LANES)[:, None]
    qq = np.arange(LANES)[None, :]
    d0 = qq - kk
    t0 = np.where(d0 >= 0, _bucket_np(np.maximum(d0, 0), n_buckets), -1)
    t1 = _bucket_np(LANES + qq - kk, n_buckets)
    return np.stack([t0, t1]).astype(np.int32)


def _attention(q_bf, k_bf, vt_bf, rel_bias, lamp, gain, batch, seq, n_heads, layer):
    t = ATTN_TILE
    nq = seq // t
    assert seq % t == 0 and t >= 2 * LANES and MAX_DISTANCE <= LANES
    m, mw = q_bf.shape
    g = ATTN_HEADS
    assert n_heads % g == 0
    bk = jnp.asarray(_attn_bucket_tiles(rel_bias.shape[0]))
    return pl.pallas_call(
        functools.partial(_attn_body, lam_init=_lambda_init(layer)),
        grid=(batch, n_heads // g, nq),
        in_specs=[pl.BlockSpec(memory_space=pltpu.SMEM),
                  pl.BlockSpec((t, g * VAL_DIM), lambda b, h, i: (b * nq + i, h)),
                  pl.BlockSpec((seq, g * VAL_DIM), lambda b, h, i: (b, h)),
                  pl.BlockSpec((None, g, nq, VAL_DIM, t), lambda b, h, i: (b, h, 0, 0, 0)),
                  pl.BlockSpec((2, LANES, LANES), lambda b, h, i: (0, 0, 0)),
                  pl.BlockSpec((4, HEAD_DIM), lambda b, h, i: (0, 0)),
                  pl.BlockSpec((1, VAL_DIM), lambda b, h, i: (0, 0))],
        out_specs=pl.BlockSpec((t, g * VAL_DIM), lambda b, h, i: (b * nq + i, h)),
        out_shape=jax.ShapeDtypeStruct((m, mw), BF16),
        scratch_shapes=[pltpu.VMEM((g, 2, VAL_DIM + 2 * SUBLANES, t), F32),
                        pltpu.VMEM((g, SUBLANES, t), F32),
                        pltpu.VMEM((g, 2, t, t), F32),
                        pltpu.VMEM((g, 2, t, t), F32),
                        pltpu.VMEM((g, 2, t, t), F32),
                        pltpu.VMEM((g, SUBLANES, t), F32), pltpu.VMEM((g, SUBLANES, t), F32)],
        compiler_params=_cparams(3),
        name="diff_attn",
    )(rel_bias, q_bf, k_bf, vt_bf, bk, lamp, gain)


def _out_body(mix_ref, qx_ref, mk_ref, mv_ref, w_ref, x_ref, o_ref):
    mw = mix_ref.shape[-1]
    xw = qx_ref.shape[-1]
    heads = [slice(c, c + XHEAD_DIM) for c in range(0, xw, XHEAD_DIM)]
    logits = [lax.dot_general(qx_ref[:, hs], mk_ref[:, hs], NT_DIMS, preferred_element_type=F32) * XSCALE
              for hs in heads]
    acc = jnp.dot(mix_ref[...], w_ref[0:mw, :], preferred_element_type=F32)
    xa = []
    for hs, s in zip(heads, logits):
        p = jnp.exp(s - jnp.max(s, axis=-1, keepdims=True))
        inv = 1.0 / jnp.sum(p, axis=-1, keepdims=True)
        xa.append((jnp.dot(p.astype(BF16), mv_ref[:, hs], preferred_element_type=F32) * inv).astype(BF16))
    acc = acc + jnp.dot(jnp.concatenate(xa, axis=-1), w_ref[mw:, :], preferred_element_type=F32)
    o_ref[...] = x_ref[...] + acc


def _out_proj(mix_bf, qx_bf, mk_bf, mv_bf, w_out_bf, x2d, layer, tm, seq):
    m, d = x2d.shape
    mw = mix_bf.shape[-1]
    xw = qx_bf.shape[-1]
    n_mem = mk_bf.shape[1] // (m // seq)
    tps = seq // tm
    row = lambda w: pl.BlockSpec((tm, w), lambda i: (i, 0))
    mem = pl.BlockSpec((None, n_mem, xw), lambda i: (layer, i // tps, 0))
    return pl.pallas_call(
        _out_body,
        grid=(m // tm,),
        in_specs=[row(mw), row(xw), mem, mem,
                  _resident((None, mw + xw, d), lambda i: (layer, 0, 0)),
                  row(d)],
        out_specs=row(d),
        out_shape=jax.ShapeDtypeStruct((m, d), F32),
        compiler_params=_cparams(1),
        name="out_proj",
    )(mix_bf, qx_bf, mk_bf, mv_bf, w_out_bf, x2d)


def _ffn_body(x_ref, g_ref, wu_ref, cw_ref, wd_ref, gf_ref, o_ref, st_ref, act_ref,
              *, tiles_per_seq, final_norm):
    tm = x_ref.shape[0] // FFN_SPLIT
    dff = wd_ref.shape[0]
    first = (pl.program_id(0) % tiles_per_seq) == 0
    n = FFN_CHUNK
    for part in range(FFN_SPLIT):
        rows = slice(part * tm, (part + 1) * tm)
        x = x_ref[rows, :]
        xn = _rms(x, g_ref[...]).astype(BF16)

        def conv_cols(lo):
            h = jnp.dot(xn, wu_ref[:, lo:lo + n], preferred_element_type=F32)
            prev = st_ref[:, lo:lo + n]
            if part == 0:
                prev = jnp.where(first, 0.0, prev)
            st_ref[:, lo:lo + n] = h[tm - SUBLANES:, :]
            return _causal_conv3(h, prev, cw_ref[:, lo:lo + n])

        for c in range(0, dff, n):
            act_ref[part, :, c:c + n] = (_silu(conv_cols(c)) * conv_cols(dff + c)).astype(BF16)
        y = x + jnp.dot(act_ref[part], wd_ref[...], preferred_element_type=F32)
        if final_norm:
            y = _rms(y, gf_ref[...])
        o_ref[rows, :] = y


def _ffn(x2d, norm, w_up_bf, ffn_conv_w, w_down_bf, norm_final, layer, tm, seq, final_norm):
    m, d = x2d.shape
    dff = w_down_bf.shape[1]
    assert dff % FFN_CHUNK == 0
    tps = seq // tm
    row = pl.BlockSpec((tm, d), lambda i: (i, 0))
    return pl.pallas_call(
        functools.partial(_ffn_body, tiles_per_seq=tps, final_norm=final_norm),
        grid=(m // tm,),
        in_specs=[row,
                  pl.BlockSpec((None, 1, d), lambda i: (layer, 0, 0)),
                  _resident((None, d, 2 * dff), lambda i: (layer, 0, 0)),
                  pl.BlockSpec((None, ffn_conv_w.shape[1], 2 * dff), lambda i: (layer, 0, 0)),
                  _resident((None, dff, d), lambda i: (layer, 0, 0)),
                  pl.BlockSpec((1, d), lambda i: (0, 0))],
        out_specs=[row, pl.BlockSpec((None, SUBLANES, 2 * dff), lambda i: (i // tps, 0, 0))],
        out_shape=[jax.ShapeDtypeStruct((m, d), F32),
                   jax.ShapeDtypeStruct((m // seq, SUBLANES, 2 * dff), F32)],
        scratch_shapes=[pltpu.VMEM((FFN_SPLIT, tm // FFN_SPLIT, dff), BF16)],
        compiler_params=_cparams(1),
        name="conv_ffn",
    )(x2d, norm, w_up_bf, ffn_conv_w, w_down_bf, norm_final)


def _proj_plain_body(x_ref, g_ref, w_ref, u_ref):
    xn = _rms(x_ref[...], g_ref[...]).astype(BF16)
    u_ref[...] = jnp.dot(xn, w_ref[...], preferred_element_type=F32)


def _proj_plain(x2d, norm, w_in_bf, layer):
    m, d = x2d.shape
    in_w = w_in_bf.shape[-1]
    n = PROJ_CHUNK
    return pl.pallas_call(
        _proj_plain_body,
        grid=(in_w // n,),
        in_specs=[pl.BlockSpec((m, d), lambda c: (0, 0)),
                  pl.BlockSpec((None, 1, d), lambda c: (layer, 0, 0)),
                  pl.BlockSpec((None, d, n), lambda c: (layer, 0, c))],
        out_specs=pl.BlockSpec((m, n), lambda c: (0, c)),
        out_shape=jax.ShapeDtypeStruct((m, in_w), F32),
        compiler_params=_cparams(1),
        name="proj_sample",
    )(x2d, norm, w_in_bf)


def _paged_attn_body(pt_ref, q_ref, kn_ref, vn_ref, bks_ref, tab_ref, lamp_ref, g_ref, *rest,
                     n_pages, lam_init):
    k_refs = rest[:n_pages]
    v_refs = rest[n_pages:2 * n_pages]
    o_ref, s_ref, bias_ref = rest[2 * n_pages:]
    rows = k_refs[0].shape[0]
    n_heads = kn_ref.shape[0]
    past = n_pages * rows
    n_buckets = tab_ref.shape[1]

    @pl.when(pl.program_id(0) == 0)
    def _build_bias():
        bk = bks_ref[...]
        out = jnp.zeros(bk.shape, F32)
        for b in range(n_buckets):
            out = jnp.where(bk == b, tab_ref[:, b:b + 1], out)
        bias_ref[...] = jnp.where(bk < 0, NEG, out)

    q = q_ref[...]
    for r in range(n_pages):
        s_ref[:, r * rows:(r + 1) * rows] = lax.dot_general(
            q, k_refs[r][...], NT_DIMS, preferred_element_type=F32)
    pad = jnp.zeros((LANES - n_heads, VAL_DIM), F32)
    s_ref[:, past:] = lax.dot_general(q, jnp.concatenate([kn_ref[...], pad], axis=0), NT_DIMS,
                                      preferred_element_type=F32)

    s = s_ref[...] + bias_ref[...]
    p = jnp.exp(s - jnp.max(s, axis=-1, keepdims=True))
    pn = p * (1.0 / jnp.sum(p, axis=-1, keepdims=True))
    pv = jnp.dot(pn[:, past:], jnp.concatenate([vn_ref[...], pad], axis=0), preferred_element_type=F32)
    for r in range(n_pages):
        pv = pv + jnp.dot(pn[:, r * rows:(r + 1) * rows], v_refs[r][...], preferred_element_type=F32)
    lam = _lambda(lamp_ref[...]) + lam_init
    o = pv[0:n_heads, :] - lam * pv[n_heads:, :]
    o_ref[...] = o * lax.rsqrt(jnp.mean(o * o, axis=-1, keepdims=True) + EPS) * g_ref[...] * (1.0 - lam_init)


def _paged_attention(q, k_new, v_new, cache_k, cache_v, page_table, rel_bias, lamp, gain, layer, attn_layer):
    db, width = q.shape
    n_layers, n_phys, page, n_heads, vd = cache_k.shape
    n_pages = page_table.shape[1]
    n_buckets = rel_bias.shape[0]
    rows = page * n_heads
    past = n_pages * rows
    assert vd == VAL_DIM == LANES and n_heads <= LANES
    ck = cache_k.reshape(n_layers, n_phys, rows, vd)
    cv = cache_v.reshape(n_layers, n_phys, rows, vd)
    q3 = q.reshape(db, 1, n_heads, vd) * QK_SCALE
    comp = (np.arange(vd) // HEAD_DIM)[None, None, None, :] == np.arange(2)[None, :, None, None]
    qrows = jnp.where(comp, q3, 0.0).reshape(db, 2 * n_heads, vd)
    col = np.arange(past + LANES)
    key = np.minimum(col // n_heads, past // n_heads)
    valid = (col < past + n_heads)[None, :] & ((col % n_heads)[None, :]
                                               == (np.arange(2 * n_heads) % n_heads)[:, None])
    bucket = _bucket_np(past // n_heads - key, n_buckets)
    bks = jnp.asarray(np.where(valid, bucket[None, :], -1).astype(np.int32))
    tab = jnp.tile(rel_bias.T, (2, 1))
    head_rows = lambda a: a.reshape(db, n_heads, vd)
    hspec = pl.BlockSpec((None, n_heads, vd), lambda b, pt: (b, 0, 0))

    def page_spec(r):
        return pl.BlockSpec((None, None, rows, vd), lambda b, pt: (attn_layer, pt[b, r], 0, 0))

    const = lambda shape: pl.BlockSpec(shape, lambda b, pt: (0,) * len(shape))
    grid_spec = pltpu.PrefetchScalarGridSpec(
        num_scalar_prefetch=1,
        grid=(db,),
        in_specs=[pl.BlockSpec((None, 2 * n_heads, vd), lambda b, pt: (b, 0, 0)), hspec, hspec,
                  const(bks.shape), const(tab.shape), const((4, HEAD_DIM)), const((1, VAL_DIM))]
                 + [page_spec(r) for r in range(n_pages)] * 2,
        out_specs=hspec,
        scratch_shapes=[pltpu.VMEM(bks.shape, F32), pltpu.VMEM(bks.shape, F32)],
    )
    out = pl.pallas_call(
        functools.partial(_paged_attn_body, n_pages=n_pages, lam_init=_lambda_init(layer)),
        grid_spec=grid_spec,
        out_shape=jax.ShapeDtypeStruct((db, n_heads, vd), F32),
        compiler_params=_cparams(1),
        name="paged_attn",
    )(page_table, qrows, head_rows(k_new), head_rows(v_new), bks, tab, lamp, gain,
      *([ck] * n_pages), *([cv] * n_pages))
    return out.reshape(db, width)


def _mem_sample_body(qx_ref, mk_ref, mv_ref, o_ref, *, n_xheads):
    g, rows, _ = mk_ref.shape
    col = lax.broadcasted_iota(jnp.int32, (SUBLANES, rows), 1)
    row = lax.broadcasted_iota(jnp.int32, (SUBLANES, rows), 0)
    own = col % n_xheads == row
    for b in range(g):
        s = lax.dot_general(qx_ref[b], mk_ref[b], NT_DIMS, preferred_element_type=F32) * XSCALE
        s = jnp.where(own, s, NEG)
        p = jnp.exp(s - jnp.max(s, axis=-1, keepdims=True))
        pn = p * (1.0 / jnp.sum(p, axis=-1, keepdims=True))
        o_ref[b] = jnp.dot(pn, mv_ref[b], preferred_element_type=F32)


def _mem_sample(qx, cache_mem_k, cache_mem_v, layer):
    db, xw = qx.shape
    depth, _, n_mem, n_xheads, xd = cache_mem_k.shape
    assert xd == XHEAD_DIM == LANES and n_xheads <= SUBLANES
    g = SUBLANES
    rows = n_mem * n_xheads
    mk = cache_mem_k.reshape(depth, db, rows, xd)
    mv = cache_mem_v.reshape(depth, db, rows, xd)
    qx8 = jnp.pad(qx.reshape(db, n_xheads, xd), ((0, 0), (0, SUBLANES - n_xheads), (0, 0)))
    mem = pl.BlockSpec((None, g, rows, xd), lambda i: (layer, i, 0, 0))
    vec = pl.BlockSpec((g, SUBLANES, xd), lambda i: (i, 0, 0))
    out = pl.pallas_call(
        functools.partial(_mem_sample_body, n_xheads=n_xheads),
        grid=(db // g,),
        in_specs=[vec, mem, mem],
        out_specs=vec,
        out_shape=jax.ShapeDtypeStruct((db, SUBLANES, xd), F32),
        compiler_params=_cparams(1),
        name="mem_attn_sample",
    )(qx8, mk, mv)
    return out[:, :n_xheads, :].reshape(db, xw)


def _out_sample_attn_body(mix_ref, xa_ref, w_ref, x_ref, y_ref):
    mw = mix_ref.shape[-1]
    acc = jnp.dot(mix_ref[...].astype(BF16), w_ref[0:mw, :], preferred_element_type=F32)
    acc = acc + jnp.dot(xa_ref[...].astype(BF16), w_ref[mw:, :], preferred_element_type=F32)
    y_ref[...] = x_ref[...] + acc


def _out_sample_conv_body(u_ref, st_ref, cw_ref, xa_ref, w_ref, x_ref, y_ref, z_ref):
    mw = z_ref.shape[-1]
    z = u_ref[:, mw:2 * mw] * u_ref[:, 2 * mw:3 * mw]
    conv = st_ref[:, 0:mw] * cw_ref[0:1, :] + st_ref[:, mw:] * cw_ref[1:2, :] + z * cw_ref[2:3, :]
    mix = u_ref[:, 0:mw] * conv
    z_ref[...] = z
    acc = jnp.dot(mix.astype(BF16), w_ref[0:mw, :], preferred_element_type=F32)
    acc = acc + jnp.dot(xa_ref[...].astype(BF16), w_ref[mw:, :], preferred_element_type=F32)
    y_ref[...] = x_ref[...] + acc


def _full(a):
    return pl.BlockSpec(a.shape, lambda i: (0,) * a.ndim)


def _out_sample_attn(mix, xa, w_out_bf, x2d, layer):
    w_spec = pl.BlockSpec((None,) + w_out_bf.shape[1:], lambda i: (layer, 0, 0))
    return pl.pallas_call(
        _out_sample_attn_body,
        grid=(1,),
        in_specs=[_full(mix), _full(xa), w_spec, _full(x2d)],
        out_specs=_full(x2d),
        out_shape=jax.ShapeDtypeStruct(x2d.shape, F32),
        compiler_params=_cparams(1),
        name="out_sample_attn",
    )(mix, xa, w_out_bf, x2d)


def _out_sample_conv(u, state2d, conv_w, xa, w_out_bf, x2d, layer, conv_layer, mw):
    db = u.shape[0]
    w_spec = pl.BlockSpec((None,) + w_out_bf.shape[1:], lambda i: (layer, 0, 0))
    cw_spec = pl.BlockSpec((None,) + conv_w.shape[1:], lambda i: (conv_layer, 0, 0))
    z_shape = jax.ShapeDtypeStruct((db, mw), F32)
    return pl.pallas_call(
        _out_sample_conv_body,
        grid=(1,),
        in_specs=[_full(u), _full(state2d), cw_spec, _full(xa), w_spec, _full(x2d)],
        out_specs=[_full(x2d), _full(z_shape)],
        out_shape=[jax.ShapeDtypeStruct(x2d.shape, F32), z_shape],
        compiler_params=_cparams(1),
        name="out_sample_conv",
    )(u, state2d, conv_w, xa, w_out_bf, x2d)


def _ffn_sample_body(x_ref, g_ref, wg_ref, wu_ref, cg_ref, cu_ref, s0g_ref, s0u_ref, s1g_ref, s1u_ref,
                     wd_ref, gf_ref, y_ref, hg_ref, hu_ref, xn_ref, *, final_norm):
    c = pl.program_id(0)

    @pl.when(c == 0)
    def _init():
        xn_ref[...] = _rms(x_ref[...], g_ref[...]).astype(BF16)
        y_ref[...] = x_ref[...]

    def step(w_ref, cw_ref, s0_ref, s1_ref, h_ref):
        h = jnp.dot(xn_ref[...], w_ref[...], preferred_element_type=F32)
        h_ref[...] = h
        return s0_ref[...] * cw_ref[0:1, :] + s1_ref[...] * cw_ref[1:2, :] + h * cw_ref[2:3, :]

    gate = step(wg_ref, cg_ref, s0g_ref, s1g_ref, hg_ref)
    up = step(wu_ref, cu_ref, s0u_ref, s1u_ref, hu_ref)
    act = (_silu(gate) * up).astype(BF16)
    y_ref[...] += jnp.dot(act, wd_ref[...], preferred_element_type=F32)

    if final_norm:
        @pl.when(c == pl.num_programs(0) - 1)
        def _finish():
            y_ref[...] = _rms(y_ref[...], gf_ref[...])


def _ffn_sample(x2d, norm, w_up_bf, ffn_conv_w, w_down_bf, state2d, norm_final, layer, final_norm):
    m, d = x2d.shape
    dff = w_down_bf.shape[1]
    n = FFN_CHUNK
    nch = dff // n
    kw = ffn_conv_w.shape[1]
    xs = pl.BlockSpec((m, d), lambda c: (0, 0))
    col = lambda off: pl.BlockSpec((m, n), lambda c: (0, off + c))
    return pl.pallas_call(
        functools.partial(_ffn_sample_body, final_norm=final_norm),
        grid=(nch,),
        in_specs=[xs,
                  pl.BlockSpec((None, 1, d), lambda c: (layer, 0, 0)),
                  pl.BlockSpec((None, d, n), lambda c: (layer, 0, c)),
                  pl.BlockSpec((None, d, n), lambda c: (layer, 0, nch + c)),
                  pl.BlockSpec((None, kw, n), lambda c: (layer, 0, c)),
                  pl.BlockSpec((None, kw, n), lambda c: (layer, 0, nch + c)),
                  col(0), col(nch), col(2 * nch), col(3 * nch),
                  pl.BlockSpec((None, n, d), lambda c: (layer, c, 0)),
                  pl.BlockSpec((1, d), lambda c: (0, 0))],
        out_specs=[xs, col(0), col(0)],
        out_shape=[jax.ShapeDtypeStruct((m, d), F32),
                   jax.ShapeDtypeStruct((m, dff), F32),
                   jax.ShapeDtypeStruct((m, dff), F32)],
        scratch_shapes=[pltpu.VMEM((m, d), BF16)],
        compiler_params=_cparams(1),
        name="conv_ffn_sample",
    )(x2d, norm, w_up_bf, w_up_bf, ffn_conv_w, ffn_conv_w, state2d, state2d, state2d, state2d,
      w_down_bf, norm_final)


def kernel(x_prompt, x_sample, mem_prompt, cache_k, cache_v, page_table, cache_mem_k, cache_mem_v,
           state_conv, state_ffn, rel_bias, w_in, w_out, norm_mix, norm_mem, w_mem_kv,
           lambda_q1, lambda_k1, lambda_q2, lambda_k2, subln_gain, conv_w,
           norm_ffn, w_up, ffn_conv_w, w_down, norm_final):
    batch, seq, d = x_prompt.shape
    db = x_sample.shape[0]
    depth = w_in.shape[0]
    n_mem = mem_prompt.shape[1]
    n_heads = cache_k.shape[3]
    mw = n_heads * VAL_DIM
    xw = w_mem_kv.shape[-1] // 2
    n_xheads = xw // XHEAD_DIM
    dff = w_down.shape[1]
    tm = min(ROW_TILE, seq)
    assert x_sample.shape[1] == 1 and seq % tm == 0 and w_in.shape[-1] == 3 * mw + xw
    assert tm == ATTN_TILE

    w_in_bf = w_in.astype(BF16)
    w_out_bf = w_out.astype(BF16)
    w_up_bf = w_up.astype(BF16)
    w_down_bf = w_down.astype(BF16)
    w_mem_bf = w_mem_kv.astype(BF16)
    norm_mix3 = norm_mix.reshape(depth, 1, d)
    norm_ffn3 = norm_ffn.reshape(depth, 1, d)
    norm_final2 = norm_final.reshape(1, d)

    xp = x_prompt.reshape(batch * seq, d)
    xs = x_sample.reshape(db, d)

    mk_f, mv_f, mk_bf, mv_bf = _memkv(mem_prompt.reshape(batch * n_mem, d), norm_mem, w_mem_bf)

    k_rows_p = v_rows_p = None
    k_rows_s, v_rows_s = [], []
    conv_st_p, conv_st_s, ffn_st_p, ffn_st_s = [], [], [], []

    for i in range(depth):
        li = i // 2
        last = i == depth - 1
        if i % 2 == 0:
            lamp = jnp.stack([lambda_q1[li], lambda_k1[li], lambda_q2[li], lambda_k2[li]])
            gain = subln_gain[li].reshape(1, VAL_DIM)
            q_bf, k_rows_p, v_rows_p, k_bf, vt_bf, qx_bf = _proj_attn(
                xp, norm_mix3, w_in_bf, i, tm, seq, mw, xw, (depth + 1) // 2, li, (k_rows_p, v_rows_p))
            mix_bf = _attention(q_bf, k_bf, vt_bf, rel_bias, lamp, gain, batch, seq, n_heads, i)
            us = _proj_plain(xs, norm_mix3, w_in_bf, i)
            qs, ks, vs = us[:, :mw], us[:, mw:2 * mw], us[:, 2 * mw:3 * mw]
            mix_s = _paged_attention(qs, ks, vs, cache_k, cache_v, page_table, rel_bias, lamp, gain, i, li)
            xa_s = _mem_sample(us[:, 3 * mw:], cache_mem_k, cache_mem_v, i)
            xs = _out_sample_attn(mix_s, xa_s, w_out_bf, xs, i)
            k_rows_s.append(ks.reshape(db, 1, n_heads, VAL_DIM))
            v_rows_s.append(vs.reshape(db, 1, n_heads, VAL_DIM))
        else:
            mix_bf, qx_bf, cst = _proj_conv(xp, norm_mix3, w_in_bf, conv_w, i, li, tm, seq, mw, xw)
            conv_st_p.append(cst[:, SUBLANES - (conv_w.shape[1] - 1):, :])
            us = _proj_plain(xs, norm_mix3, w_in_bf, i)
            xa_s = _mem_sample(us[:, 3 * mw:], cache_mem_k, cache_mem_v, i)
            st = state_conv[li]
            xs, z_s = _out_sample_conv(us, st.reshape(db, -1), conv_w, xa_s, w_out_bf, xs, i, li, mw)
            conv_st_s.append(jnp.stack([st[:, 1, :], z_s], axis=1))

        xp = _out_proj(mix_bf, qx_bf, mk_bf, mv_bf, w_out_bf, xp, i, tm, seq)
        xp, fst = _ffn(xp, norm_ffn3, w_up_bf, ffn_conv_w, w_down_bf, norm_final2, i,
                       min(FFN_TILE, seq), seq, last)
        ffn_st_p.append(fst[:, SUBLANES - (ffn_conv_w.shape[1] - 1):, :])

        fs = state_ffn[i]
        xs, hg, hu = _ffn_sample(xs, norm_ffn3, w_up_bf, ffn_conv_w, w_down_bf, fs.reshape(db, -1),
                                 norm_final2, i, last)
        ffn_st_s.append(jnp.stack([fs[:, 1, :], jnp.concatenate([hg, hu], axis=-1)], axis=1))

    mem_shape = (depth, batch, n_mem, n_xheads, XHEAD_DIM)
    rows_shape = (-1, batch, seq, n_heads, VAL_DIM)
    return (xp.reshape(batch, seq, d), xs.reshape(db, 1, d),
            k_rows_p.reshape(rows_shape), v_rows_p.reshape(rows_shape), jnp.stack(k_rows_s), jnp.stack(v_rows_s),
            mk_f.reshape(mem_shape), mv_f.reshape(mem_shape),
            jnp.stack(conv_st_p), jnp.stack(conv_st_s), jnp.stack(ffn_st_p), jnp.stack(ffn_st_s))
```

```python
import functools
import math

import numpy as np
import jax
import jax.numpy as jnp
from jax import lax
from jax.experimental import pallas as pl
from jax.experimental.pallas import tpu as pltpu

F32 = jnp.float32
BF16 = jnp.bfloat16

HEAD_DIM = 64
VAL_DIM = 2 * HEAD_DIM
XHEAD_DIM = 128
MAX_DISTANCE = 128
EPS = 1e-6
QK_SCALE = HEAD_DIM ** -0.5
XSCALE = XHEAD_DIM ** -0.5
LOG2E = math.log2(math.e)
NEG = -1e30

LANES = 128
SUBLANES = 8
VMEM_LIMIT_MB = 56

ATTN_TILE = 512
ATTN_STRIP = 256
ATTN_HEADS = 2
ROW_TILE = 512
FFN_CHUNK = 256
FFN_TILE = 512
FFN_SPLIT = 1
PROJ_CHUNK = 512

NT_DIMS = (((1,), (1,)), ((), ()))


def _cparams(n_axes):
    return pltpu.CompilerParams(dimension_semantics=("arbitrary",) * n_axes,
                                vmem_limit_bytes=VMEM_LIMIT_MB * 1024 * 1024)


def _resident(block_shape, index_map):
    return pl.BlockSpec(block_shape, index_map, pipeline_mode=pl.Buffered(1))


def _rms(x, g):
    return x * lax.rsqrt(jnp.mean(x * x, axis=-1, keepdims=True) + EPS) * g


def _lambda_init(layer_idx):
    return 0.8 - 0.6 * math.exp(-0.3 * layer_idx)


def _lambda(lamp):
    a = jnp.sum(lamp[0:1] * lamp[1:2], axis=-1, keepdims=True)
    b = jnp.sum(lamp[2:3] * lamp[3:4], axis=-1, keepdims=True)
    return jnp.exp(a) - jnp.exp(b)


def _bucket_np(n, n_buckets):
    max_exact = n_buckets // 2
    nf = np.maximum(n, 1).astype(np.float32)
    large = max_exact + (np.log(nf / np.float32(max_exact)) / np.float32(math.log(MAX_DISTANCE / max_exact))
                         * np.float32(n_buckets - max_exact)).astype(np.int32)
    large = np.minimum(large, n_buckets - 1)
    return np.where(n < max_exact, n, large).astype(np.int32)


def _causal_conv3(z, prev, w):
    def taps(a):
        return pltpu.roll(a, 2, 0) * w[0:1, :] + pltpu.roll(a, 1, 0) * w[1:2, :] + a * w[2:3, :]

    head = taps(jnp.concatenate([prev, z[0:SUBLANES, :]], axis=0))[SUBLANES:, :]
    return jnp.concatenate([head, taps(z)[SUBLANES:, :]], axis=0)


def _silu(g):
    return g * (1.0 / (1.0 + jnp.exp(-g)))


def _memkv_body(mem_ref, g_ref, w_ref, kf_ref, vf_ref, kb_ref, vb_ref):
    xw = kf_ref.shape[-1]
    xn = _rms(mem_ref[...], g_ref[...]).astype(BF16)
    k = jnp.dot(xn, w_ref[:, :xw], preferred_element_type=F32)
    v = jnp.dot(xn, w_ref[:, xw:], preferred_element_type=F32)
    kf_ref[...] = k
    vf_ref[...] = v
    kb_ref[...] = k.astype(BF16)
    vb_ref[...] = v.astype(BF16)


def _memkv(mem2d, norm_mem, w_mem_bf):
    depth, d, two_xw = w_mem_bf.shape
    xw = two_xw // 2
    rows = mem2d.shape[0]
    out = lambda dt: jax.ShapeDtypeStruct((depth, rows, xw), dt)
    ospec = pl.BlockSpec((None, rows, xw), lambda l: (l, 0, 0))
    return pl.pallas_call(
        _memkv_body,
        grid=(depth,),
        in_specs=[pl.BlockSpec((rows, d), lambda l: (0, 0)),
                  pl.BlockSpec((None, 1, d), lambda l: (l, 0, 0)),
                  pl.BlockSpec((None, d, two_xw), lambda l: (l, 0, 0))],
        out_specs=[ospec, ospec, ospec, ospec],
        out_shape=[out(F32), out(F32), out(BF16), out(BF16)],
        compiler_params=_cparams(1),
        name="mem_kv",
    )(mem2d, norm_mem.reshape(depth, 1, d), w_mem_bf)


def _proj_attn_body(x_ref, g_ref, w_ref, *refs, first):
    if first:
        q_ref, kf_ref, vf_ref, kb_ref, vt_ref, qx_ref = refs
        for stack in (kf_ref, vf_ref):
            for other in range(1, stack.shape[0]):
                stack[other] = jnp.zeros(stack.shape[1:], F32)
        kf_ref, vf_ref = kf_ref.at[0], vf_ref.at[0]
    else:
        _, _, q_ref, kf_ref, vf_ref, kb_ref, vt_ref, qx_ref = refs
    mw = q_ref.shape[-1]
    xw = qx_ref.shape[-1]
    xn = _rms(x_ref[...], g_ref[...]).astype(BF16)

    def mm(lo, n):
        return jnp.dot(xn, w_ref[:, lo:lo + n], preferred_element_type=F32)

    n = PROJ_CHUNK
    for c in range(0, mw, n):
        q_ref[:, c:c + n] = (mm(c, n) * (QK_SCALE * LOG2E)).astype(BF16)
        k = mm(mw + c, n)
        kf_ref[:, c:c + n] = k
        kb_ref[:, c:c + n] = k.astype(BF16)
        v = mm(2 * mw + c, n)
        vf_ref[:, c:c + n] = v
        for hc in range(0, n, VAL_DIM):
            vt_ref[(c + hc) // VAL_DIM] = v[:, hc:hc + VAL_DIM].T.astype(BF16)
    qx_ref[...] = mm(3 * mw, xw).astype(BF16)


def _proj_attn(x2d, norm, w_in_bf, layer, tm, seq, mw, xw, n_slabs, slab, stacks):
    m, d = x2d.shape
    in_w = w_in_bf.shape[-1]
    tps = seq // tm
    n_heads = mw // VAL_DIM
    first = slab == 0
    row = lambda w: pl.BlockSpec((tm, w), lambda i: (i, 0))
    if first:
        stack_spec = pl.BlockSpec((n_slabs, tm, mw), lambda i: (0, i, 0))
        extra_specs, extra_args, aliases = [], (), {}
    else:
        stack_spec = pl.BlockSpec((None, tm, mw), lambda i: (slab, i, 0))
        extra_specs = [pl.BlockSpec(memory_space=pl.ANY)] * 2
        extra_args, aliases = tuple(stacks), {3: 1, 4: 2}
    stack_shape = jax.ShapeDtypeStruct((n_slabs, m, mw), F32)
    return pl.pallas_call(
        functools.partial(_proj_attn_body, first=first),
        grid=(m // tm,),
        in_specs=[row(d),
                  pl.BlockSpec((None, 1, d), lambda i: (layer, 0, 0)),
                  _resident((None, d, in_w), lambda i: (layer, 0, 0))] + extra_specs,
        out_specs=[row(mw), stack_spec, stack_spec, row(mw),
                   pl.BlockSpec((None, n_heads, None, VAL_DIM, tm), lambda i: (i // tps, 0, i % tps, 0, 0)),
                   row(xw)],
        out_shape=[jax.ShapeDtypeStruct((m, mw), BF16), stack_shape, stack_shape,
                   jax.ShapeDtypeStruct((m, mw), BF16),
                   jax.ShapeDtypeStruct((m // seq, n_heads, tps, VAL_DIM, tm), BF16),
                   jax.ShapeDtypeStruct((m, xw), BF16)],
        input_output_aliases=aliases,
        compiler_params=_cparams(1),
        name="proj_attn",
    )(x2d, norm, w_in_bf, *extra_args)


def _proj_conv_body(x_ref, g_ref, w_ref, cw_ref, mix_ref, qx_ref, st_ref, *, tiles_per_seq):
    mw = mix_ref.shape[-1]
    xw = qx_ref.shape[-1]
    tm = x_ref.shape[0]
    first = (pl.program_id(0) % tiles_per_seq) == 0
    xn = _rms(x_ref[...], g_ref[...]).astype(BF16)

    def mm(lo, n):
        return jnp.dot(xn, w_ref[:, lo:lo + n], preferred_element_type=F32)

    n = PROJ_CHUNK
    for c in range(0, mw, n):
        z = mm(mw + c, n) * mm(2 * mw + c, n)
        prev = jnp.where(first, 0.0, st_ref[:, c:c + n])
        y = _causal_conv3(z, prev, cw_ref[:, c:c + n])
        st_ref[:, c:c + n] = z[tm - SUBLANES:, :]
        mix_ref[:, c:c + n] = (mm(c, n) * y).astype(BF16)
    qx_ref[...] = mm(3 * mw, xw).astype(BF16)


def _proj_conv(x2d, norm, w_in_bf, conv_w, layer, conv_layer, tm, seq, mw, xw):
    m, d = x2d.shape
    in_w = w_in_bf.shape[-1]
    tps = seq // tm
    row = lambda w: pl.BlockSpec((tm, w), lambda i: (i, 0))
    return pl.pallas_call(
        functools.partial(_proj_conv_body, tiles_per_seq=tps),
        grid=(m // tm,),
        in_specs=[row(d),
                  pl.BlockSpec((None, 1, d), lambda i: (layer, 0, 0)),
                  _resident((None, d, in_w), lambda i: (layer, 0, 0)),
                  pl.BlockSpec((None, conv_w.shape[1], mw), lambda i: (conv_layer, 0, 0))],
        out_specs=[row(mw), row(xw),
                   pl.BlockSpec((None, SUBLANES, mw), lambda i: (i // tps, 0, 0))],
        out_shape=[jax.ShapeDtypeStruct((m, mw), BF16),
                   jax.ShapeDtypeStruct((m, xw), BF16),
                   jax.ShapeDtypeStruct((m // seq, SUBLANES, mw), F32)],
        compiler_params=_cparams(1),
        name="proj_conv",
    )(x2d, norm, w_in_bf, conv_w)


def _attn_body(tab_ref, q_ref, k_ref, vt_ref, bk_ref, lamp_ref, g_ref, o_ref,
               acc, mrun, bias2, buf_a, buf_b, bm_a, bm_b, *, lam_init):
    t = q_ref.shape[0]
    n_buckets = tab_ref.shape[0]
    hp = pl.program_id(1)
    qi = pl.program_id(2)
    nb = t // LANES
    heads = range(ATTN_HEADS)
    hcols = [slice(hh * VAL_DIM, (hh + 1) * VAL_DIM) for hh in heads]
    strips = [slice(c, c + ATTN_STRIP) for c in range(0, t, ATTN_STRIP)]

    @pl.when(qi == 0)
    def _build_bias():
        for hh in heads:
            h = hp * ATTN_HEADS + hh
            far = tab_ref[n_buckets - 1, h]

            def tile(bk):
                out = jnp.zeros(bk.shape, F32)
                for b in range(n_buckets - 1):
                    out = jnp.where(bk == b, (tab_ref[b, h] - far) * LOG2E, out)
                return jnp.where(bk < 0, NEG, out)

            d0 = tile(bk_ref[0])
            d1 = tile(bk_ref[1])
            bias2[hh] = jnp.zeros(bias2.shape[1:], F32)
            bias2[hh, 0, (nb - 1) * LANES:, 0:LANES] = d1
            for r in range(nb):
                for c in range(nb):
                    sub = (hh, 1, slice(r * LANES, (r + 1) * LANES), slice(c * LANES, (c + 1) * LANES))
                    if c < r:
                        bias2[sub] = jnp.full((LANES, LANES), NEG, F32)
                    elif c == r:
                        bias2[sub] = d0
                    elif c == r + 1:
                        bias2[sub] = d1

    lane = lax.broadcasted_iota(jnp.int32, (t, VAL_DIM), 1)
    qz = []
    for hh in heads:
        q = q_ref[:, hcols[hh]]
        zero = jnp.zeros_like(q)
        qz.append((jnp.where(lane < HEAD_DIM, q, zero), jnp.where(lane >= HEAD_DIM, q, zero)))
    ones = jnp.ones((2 * SUBLANES, t), BF16)

    acc[...] = jnp.zeros(acc.shape, F32)
    mrun[...] = jnp.full(mrun.shape, NEG, F32)

    PREV, DIAG = 0, 1

    def n_keys(cs, diag):
        return cs.stop if diag else t

    def logits_steps(j, buf, bias_idx=None):
        sbuf, bm = buf

        def step(hh, comp, cs):
            keys = n_keys(cs, bias_idx == DIAG)
            kb = k_ref[pl.ds(pl.multiple_of(j * t, t), keys), hcols[hh]]
            s = lax.dot_general(kb, qz[hh][comp][cs, :], NT_DIMS, preferred_element_type=F32)
            if bias_idx is not None:
                s = s + bias2[hh, bias_idx, 0:keys, cs]
            sbuf[hh, comp, 0:keys, cs] = s
            bm[hh, comp:comp + 1, cs] = jnp.max(s, axis=0, keepdims=True)

        return [functools.partial(step, hh, comp, cs) for hh in heads for comp in range(2) for cs in strips]

    def softmax_steps(j, buf, diag=False):
        sbuf, bm = buf

        def step(hh, comp, cs):
            keys = n_keys(cs, diag)
            vt = jnp.concatenate([vt_ref[hh, j], ones], axis=0)
            m_old = mrun[hh, comp:comp + 1, cs]
            m_new = jnp.maximum(m_old, bm[hh, comp:comp + 1, cs])
            alpha = jnp.exp2(m_old - m_new)
            mrun[hh, comp:comp + 1, cs] = m_new
            p = jnp.exp2(sbuf[hh, comp, 0:keys, cs] - m_new).astype(BF16)
            acc[hh, comp, :, cs] = (alpha * acc[hh, comp, :, cs]
                                    + jnp.dot(vt[:, 0:keys], p, preferred_element_type=F32))

        return [functools.partial(step, hh, comp, cs) for hh in heads for comp in range(2) for cs in strips]

    def run(*step_lists):
        for group in zip(*step_lists):
            for step in reversed(group):
                step()

    buf_a = (buf_a, bm_a)
    buf_b = (buf_b, bm_b)

    @pl.when(qi == 0)
    def _one_block():
        run(logits_steps(0, buf_a, DIAG))
        run(softmax_steps(0, buf_a, diag=True))

    @pl.when(qi == 1)
    def _two_blocks():
        run(logits_steps(0, buf_a, PREV))
        run(softmax_steps(0, buf_a), logits_steps(1, buf_b, DIAG))
        run(softmax_steps(1, buf_b, diag=True))

    @pl.when(qi >= 2)
    def _bias_free_blocks():
        n_pairs = (qi - 2) // 2
        run(logits_steps(0, buf_a))

        def pair(i, carry):
            j = 2 * i
            run(softmax_steps(j, buf_a), logits_steps(j + 1, buf_b))
            run(softmax_steps(j + 1, buf_b), logits_steps(j + 2, buf_a))
            return carry

        lax.fori_loop(0, n_pairs, pair, 0)

    @pl.when((qi >= 2) & (qi % 2 == 0))
    def _even_tail():
        run(softmax_steps(qi - 2, buf_a), logits_steps(qi - 1, buf_b, PREV))
        run(softmax_steps(qi - 1, buf_b), logits_steps(qi, buf_a, DIAG))
        run(softmax_steps(qi, buf_a, diag=True))

    @pl.when((qi >= 2) & (qi % 2 == 1))
    def _odd_tail():
        run(softmax_steps(qi - 3, buf_a), logits_steps(qi - 2, buf_b))
        run(softmax_steps(qi - 2, buf_b), logits_steps(qi - 1, buf_a, PREV))
        run(softmax_steps(qi - 1, buf_a), logits_steps(qi, buf_b, DIAG))
        run(softmax_steps(qi, buf_b, diag=True))

    lam = _lambda(lamp_ref[...]) + lam_init
    for hh in heads:
        inv1 = 1.0 / acc[hh, 0, VAL_DIM:VAL_DIM + 1, :]
        inv2 = 1.0 / acc[hh, 1, VAL_DIM:VAL_DIM + 1, :]
        o_t = acc[hh, 0, 0:VAL_DIM, :] * inv1 - lam * (acc[hh, 1, 0:VAL_DIM, :] * inv2)
        o = o_t.T
        y = o * lax.rsqrt(jnp.mean(o * o, axis=-1, keepdims=True) + EPS) * g_ref[...] * (1.0 - lam_init)
        o_ref[:, hcols[hh]] = y.astype(o_ref.dtype)


def _attn_bucket_tiles(n_buckets):
    kk = np.arange(LANES)[:, None]
    qq = np.arange(LANES)[None, :]
    d0 = qq - kk
    t0 = np.where(d0 >= 0, _bucket_np(np.maximum(d0, 0), n_buckets), -1)
    t1 = _bucket_np(LANES + qq - kk, n_buckets)
    return np.stack([t0, t1]).astype(np.int32)


def _attention(q_bf, k_bf, vt_bf, rel_bias, lamp, gain, batch, seq, n_heads, layer):
    t = ATTN_TILE
    nq = seq // t
    assert seq % t == 0 and t >= 2 * LANES and MAX_DISTANCE <= LANES
    m, mw = q_bf.shape
    g = ATTN_HEADS
    assert n_heads % g == 0
    bk = jnp.asarray(_attn_bucket_tiles(rel_bias.shape[0]))
    return pl.pallas_call(
        functools.partial(_attn_body, lam_init=_lambda_init(layer)),
        grid=(batch, n_heads // g, nq),
        in_specs=[pl.BlockSpec(memory_space=pltpu.SMEM),
                  pl.BlockSpec((t, g * VAL_DIM), lambda b, h, i: (b * nq + i, h)),
                  pl.BlockSpec((seq, g * VAL_DIM), lambda b, h, i: (b, h)),
                  pl.BlockSpec((None, g, nq, VAL_DIM, t), lambda b, h, i: (b, h, 0, 0, 0)),
                  pl.BlockSpec((2, LANES, LANES), lambda b, h, i: (0, 0, 0)),
                  pl.BlockSpec((4, HEAD_DIM), lambda b, h, i: (0, 0)),
                  pl.BlockSpec((1, VAL_DIM), lambda b, h, i: (0, 0))],
        out_specs=pl.BlockSpec((t, g * VAL_DIM), lambda b, h, i: (b * nq + i, h)),
        out_shape=jax.ShapeDtypeStruct((m, mw), BF16),
        scratch_shapes=[pltpu.VMEM((g, 2, VAL_DIM + 2 * SUBLANES, t), F32),
                        pltpu.VMEM((g, SUBLANES, t), F32),
                        pltpu.VMEM((g, 2, t, t), F32),
                        pltpu.VMEM((g, 2, t, t), F32),
                        pltpu.VMEM((g, 2, t, t), F32),
                        pltpu.VMEM((g, SUBLANES, t), F32), pltpu.VMEM((g, SUBLANES, t), F32)],
        compiler_params=_cparams(3),
        name="diff_attn",
    )(rel_bias, q_bf, k_bf, vt_bf, bk, lamp, gain)


def _out_body(mix_ref, qx_ref, mk_ref, mv_ref, w_ref, x_ref, o_ref):
    mw = mix_ref.shape[-1]
    xw = qx_ref.shape[-1]
    heads = [slice(c, c + XHEAD_DIM) for c in range(0, xw, XHEAD_DIM)]
    logits = [lax.dot_general(qx_ref[:, hs], mk_ref[:, hs], NT_DIMS, preferred_element_type=F32) * XSCALE
              for hs in heads]
    acc = jnp.dot(mix_ref[...], w_ref[0:mw, :], preferred_element_type=F32)
    xa = []
    for hs, s in zip(heads, logits):
        p = jnp.exp(s - jnp.max(s, axis=-1, keepdims=True))
        inv = 1.0 / jnp.sum(p, axis=-1, keepdims=True)
        xa.append((jnp.dot(p.astype(BF16), mv_ref[:, hs], preferred_element_type=F32) * inv).astype(BF16))
    acc = acc + jnp.dot(jnp.concatenate(xa, axis=-1), w_ref[mw:, :], preferred_element_type=F32)
    o_ref[...] = x_ref[...] + acc


def _out_proj(mix_bf, qx_bf, mk_bf, mv_bf, w_out_bf, x2d, layer, tm, seq):
    m, d = x2d.shape
    mw = mix_bf.shape[-1]
    xw = qx_bf.shape[-1]
    n_mem = mk_bf.shape[1] // (m // seq)
    tps = seq // tm
    row = lambda w: pl.BlockSpec((tm, w), lambda i: (i, 0))
    mem = pl.BlockSpec((None, n_mem, xw), lambda i: (layer, i // tps, 0))
    return pl.pallas_call(
        _out_body,
        grid=(m // tm,),
        in_specs=[row(mw), row(xw), mem, mem,
                  _resident((None, mw + xw, d), lambda i: (layer, 0, 0)),
                  row(d)],
        out_specs=row(d),
        out_shape=jax.ShapeDtypeStruct((m, d), F32),
        compiler_params=_cparams(1),
        name="out_proj",
    )(mix_bf, qx_bf, mk_bf, mv_bf, w_out_bf, x2d)


def _ffn_body(x_ref, g_ref, wu_ref, cw_ref, wd_ref, gf_ref, o_ref, st_ref, act_ref,
              *, tiles_per_seq, final_norm):
    tm = x_ref.shape[0] // FFN_SPLIT
    dff = wd_ref.shape[0]
    first = (pl.program_id(0) % tiles_per_seq) == 0
    n = FFN_CHUNK
    for part in range(FFN_SPLIT):
        rows = slice(part * tm, (part + 1) * tm)
        x = x_ref[rows, :]
        xn = _rms(x, g_ref[...]).astype(BF16)

        def conv_cols(lo):
            h = jnp.dot(xn, wu_ref[:, lo:lo + n], preferred_element_type=F32)
            prev = st_ref[:, lo:lo + n]
            if part == 0:
                prev = jnp.where(first, 0.0, prev)
            st_ref[:, lo:lo + n] = h[tm - SUBLANES:, :]
            return _causal_conv3(h, prev, cw_ref[:, lo:lo + n])

        for c in range(0, dff, n):
            act_ref[part, :, c:c + n] = (_silu(conv_cols(c)) * conv_cols(dff + c)).astype(BF16)
        y = x + jnp.dot(act_ref[part], wd_ref[...], preferred_element_type=F32)
        if final_norm:
            y = _rms(y, gf_ref[...])
        o_ref[rows, :] = y


def _ffn(x2d, norm, w_up_bf, ffn_conv_w, w_down_bf, norm_final, layer, tm, seq, final_norm):
    m, d = x2d.shape
    dff = w_down_bf.shape[1]
    assert dff % FFN_CHUNK == 0
    tps = seq // tm
    row = pl.BlockSpec((tm, d), lambda i: (i, 0))
    return pl.pallas_call(
        functools.partial(_ffn_body, tiles_per_seq=tps, final_norm=final_norm),
        grid=(m // tm,),
        in_specs=[row,
                  pl.BlockSpec((None, 1, d), lambda i: (layer, 0, 0)),
                  _resident((None, d, 2 * dff), lambda i: (layer, 0, 0)),
                  pl.BlockSpec((None, ffn_conv_w.shape[1], 2 * dff), lambda i: (layer, 0, 0)),
                  _resident((None, dff, d), lambda i: (layer, 0, 0)),
                  pl.BlockSpec((1, d), lambda i: (0, 0))],
        out_specs=[row, pl.BlockSpec((None, SUBLANES, 2 * dff), lambda i: (i // tps, 0, 0))],
        out_shape=[jax.ShapeDtypeStruct((m, d), F32),
                   jax.ShapeDtypeStruct((m // seq, SUBLANES, 2 * dff), F32)],
        scratch_shapes=[pltpu.VMEM((FFN_SPLIT, tm // FFN_SPLIT, dff), BF16)],
        compiler_params=_cparams(1),
        name="conv_ffn",
    )(x2d, norm, w_up_bf, ffn_conv_w, w_down_bf, norm_final)


def _proj_plain_body(x_ref, g_ref, w_ref, u_ref):
    xn = _rms(x_ref[...], g_ref[...]).astype(BF16)
    u_ref[...] = jnp.dot(xn, w_ref[...], preferred_element_type=F32)


def _proj_plain(x2d, norm, w_in_bf, layer):
    m, d = x2d.shape
    in_w = w_in_bf.shape[-1]
    n = PROJ_CHUNK
    return pl.pallas_call(
        _proj_plain_body,
        grid=(in_w // n,),
        in_specs=[pl.BlockSpec((m, d), lambda c: (0, 0)),
                  pl.BlockSpec((None, 1, d), lambda c: (layer, 0, 0)),
                  pl.BlockSpec((None, d, n), lambda c: (layer, 0, c))],
        out_specs=pl.BlockSpec((m, n), lambda c: (0, c)),
        out_shape=jax.ShapeDtypeStruct((m, in_w), F32),
        compiler_params=_cparams(1),
        name="proj_sample",
    )(x2d, norm, w_in_bf)


def _paged_attn_body(pt_ref, q_ref, kn_ref, vn_ref, bks_ref, tab_ref, lamp_ref, g_ref, *rest,
                     n_pages, lam_init):
    k_refs = rest[:n_pages]
    v_refs = rest[n_pages:2 * n_pages]
    o_ref, s_ref, bias_ref = rest[2 * n_pages:]
    rows = k_refs[0].shape[0]
    n_heads = kn_ref.shape[0]
    past = n_pages * rows
    n_buckets = tab_ref.shape[1]

    @pl.when(pl.program_id(0) == 0)
    def _build_bias():
        bk = bks_ref[...]
        out = jnp.zeros(bk.shape, F32)
        for b in range(n_buckets):
            out = jnp.where(bk == b, tab_ref[:, b:b + 1], out)
        bias_ref[...] = jnp.where(bk < 0, NEG, out)

    q = q_ref[...]
    for r in range(n_pages):
        s_ref[:, r * rows:(r + 1) * rows] = lax.dot_general(
            q, k_refs[r][...], NT_DIMS, preferred_element_type=F32)
    pad = jnp.zeros((LANES - n_heads, VAL_DIM), F32)
    s_ref[:, past:] = lax.dot_general(q, jnp.concatenate([kn_ref[...], pad], axis=0), NT_DIMS,
                                      preferred_element_type=F32)

    s = s_ref[...] + bias_ref[...]
    p = jnp.exp(s - jnp.max(s, axis=-1, keepdims=True))
    pn = p * (1.0 / jnp.sum(p, axis=-1, keepdims=True))
    pv = jnp.dot(pn[:, past:], jnp.concatenate([vn_ref[...], pad], axis=0), preferred_element_type=F32)
    for r in range(n_pages):
        pv = pv + jnp.dot(pn[:, r * rows:(r + 1) * rows], v_refs[r][...], preferred_element_type=F32)
    lam = _lambda(lamp_ref[...]) + lam_init
    o = pv[0:n_heads, :] - lam * pv[n_heads:, :]
    o_ref[...] = o * lax.rsqrt(jnp.mean(o * o, axis=-1, keepdims=True) + EPS) * g_ref[...] * (1.0 - lam_init)


def _paged_attention(q, k_new, v_new, cache_k, cache_v, page_table, rel_bias, lamp, gain, layer, attn_layer):
    db, width = q.shape
    n_layers, n_phys, page, n_heads, vd = cache_k.shape
    n_pages = page_table.shape[1]
    n_buckets = rel_bias.shape[0]
    rows = page * n_heads
    past = n_pages * rows
    assert vd == VAL_DIM == LANES and n_heads <= LANES
    ck = cache_k.reshape(n_layers, n_phys, rows, vd)
    cv = cache_v.reshape(n_layers, n_phys, rows, vd)
    q3 = q.reshape(db, 1, n_heads, vd) * QK_SCALE
    comp = (np.arange(vd) // HEAD_DIM)[None, None, None, :] == np.arange(2)[None, :, None, None]
    qrows = jnp.where(comp, q3, 0.0).reshape(db, 2 * n_heads, vd)
    col = np.arange(past + LANES)
    key = np.minimum(col // n_heads, past // n_heads)
    valid = (col < past + n_heads)[None, :] & ((col % n_heads)[None, :]
                                               == (np.arange(2 * n_heads) % n_heads)[:, None])
    bucket = _bucket_np(past // n_heads - key, n_buckets)
    bks = jnp.asarray(np.where(valid, bucket[None, :], -1).astype(np.int32))
    tab = jnp.tile(rel_bias.T, (2, 1))
    head_rows = lambda a: a.reshape(db, n_heads, vd)
    hspec = pl.BlockSpec((None, n_heads, vd), lambda b, pt: (b, 0, 0))

    def page_spec(r):
        return pl.BlockSpec((None, None, rows, vd), lambda b, pt: (attn_layer, pt[b, r], 0, 0))

    const = lambda shape: pl.BlockSpec(shape, lambda b, pt: (0,) * len(shape))
    grid_spec = pltpu.PrefetchScalarGridSpec(
        num_scalar_prefetch=1,
        grid=(db,),
        in_specs=[pl.BlockSpec((None, 2 * n_heads, vd), lambda b, pt: (b, 0, 0)), hspec, hspec,
                  const(bks.shape), const(tab.shape), const((4, HEAD_DIM)), const((1, VAL_DIM))]
                 + [page_spec(r) for r in range(n_pages)] * 2,
        out_specs=hspec,
        scratch_shapes=[pltpu.VMEM(bks.shape, F32), pltpu.VMEM(bks.shape, F32)],
    )
    out = pl.pallas_call(
        functools.partial(_paged_attn_body, n_pages=n_pages, lam_init=_lambda_init(layer)),
        grid_spec=grid_spec,
        out_shape=jax.ShapeDtypeStruct((db, n_heads, vd), F32),
        compiler_params=_cparams(1),
        name="paged_attn",
    )(page_table, qrows, head_rows(k_new), head_rows(v_new), bks, tab, lamp, gain,
      *([ck] * n_pages), *([cv] * n_pages))
    return out.reshape(db, width)


def _mem_sample_body(qx_ref, mk_ref, mv_ref, o_ref):
    g, n_mem, n_xheads, xd = mk_ref.shape
    rows = n_mem * n_xheads
    col = lax.broadcasted_iota(jnp.int32, (SUBLANES, rows), 1)
    row = lax.broadcasted_iota(jnp.int32, (SUBLANES, rows), 0)
    own = col % n_xheads == row
    logits = [jnp.where(own, lax.dot_general(qx_ref[b], mk_ref[b].reshape(rows, xd), NT_DIMS,
                                             preferred_element_type=F32) * XSCALE, NEG)
              for b in range(g)]
    weights = []
    for s in logits:
        p = jnp.exp(s - jnp.max(s, axis=-1, keepdims=True))
        weights.append(p * (1.0 / jnp.sum(p, axis=-1, keepdims=True)))
    for b, pn in enumerate(weights):
        o_ref[b] = jnp.dot(pn, mv_ref[b].reshape(rows, xd), preferred_element_type=F32)


def _mem_sample(qx, cache_mem_k, cache_mem_v, layer):
    db, xw = qx.shape
    depth, _, n_mem, n_xheads, xd = cache_mem_k.shape
    assert xd == XHEAD_DIM == LANES and n_xheads <= SUBLANES
    g = SUBLANES
    qx8 = jnp.pad(qx.reshape(db, n_xheads, xd), ((0, 0), (0, SUBLANES - n_xheads), (0, 0)))
    mem = pl.BlockSpec((None, g, n_mem, n_xheads, xd), lambda i: (layer, i, 0, 0, 0))
    vec = pl.BlockSpec((g, SUBLANES, xd), lambda i: (i, 0, 0))
    out = pl.pallas_call(
        _mem_sample_body,
        grid=(db // g,),
        in_specs=[vec, mem, mem],
        out_specs=vec,
        out_shape=jax.ShapeDtypeStruct((db, SUBLANES, xd), F32),
        compiler_params=_cparams(1),
        name="mem_attn_sample",
    )(qx8, cache_mem_k, cache_mem_v)
    return out[:, :n_xheads, :].reshape(db, xw)


def _out_sample_attn_body(mix_ref, xa_ref, w_ref, x_ref, y_ref):
    mw = mix_ref.shape[-1]
    acc = jnp.dot(mix_ref[...].astype(BF16), w_ref[0:mw, :], preferred_element_type=F32)
    acc = acc + jnp.dot(xa_ref[...].astype(BF16), w_ref[mw:, :], preferred_element_type=F32)
    y_ref[...] = x_ref[...] + acc


def _out_sample_conv_body(u_ref, st_ref, cw_ref, xa_ref, w_ref, x_ref, y_ref, z_ref):
    mw = z_ref.shape[-1]
    z = u_ref[:, mw:2 * mw] * u_ref[:, 2 * mw:3 * mw]
    conv = st_ref[:, 0:mw] * cw_ref[0:1, :] + st_ref[:, mw:] * cw_ref[1:2, :] + z * cw_ref[2:3, :]
    mix = u_ref[:, 0:mw] * conv
    z_ref[...] = z
    acc = jnp.dot(mix.astype(BF16), w_ref[0:mw, :], preferred_element_type=F32)
    acc = acc + jnp.dot(xa_ref[...].astype(BF16), w_ref[mw:, :], preferred_element_type=F32)
    y_ref[...] = x_ref[...] + acc


def _full(a):
    return pl.BlockSpec(a.shape, lambda i: (0,) * a.ndim)


def _out_sample_attn(mix, xa, w_out_bf, x2d, layer):
    w_spec = pl.BlockSpec((None,) + w_out_bf.shape[1:], lambda i: (layer, 0, 0))
    return pl.pallas_call(
        _out_sample_attn_body,
        grid=(1,),
        in_specs=[_full(mix), _full(xa), w_spec, _full(x2d)],
        out_specs=_full(x2d),
        out_shape=jax.ShapeDtypeStruct(x2d.shape, F32),
        compiler_params=_cparams(1),
        name="out_sample_attn",
    )(mix, xa, w_out_bf, x2d)


def _out_sample_conv(u, state2d, conv_w, xa, w_out_bf, x2d, layer, conv_layer, mw):
    db = u.shape[0]
    w_spec = pl.BlockSpec((None,) + w_out_bf.shape[1:], lambda i: (layer, 0, 0))
    cw_spec = pl.BlockSpec((None,) + conv_w.shape[1:], lambda i: (conv_layer, 0, 0))
    z_shape = jax.ShapeDtypeStruct((db, mw), F32)
    return pl.pallas_call(
        _out_sample_conv_body,
        grid=(1,),
        in_specs=[_full(u), _full(state2d), cw_spec, _full(xa), w_spec, _full(x2d)],
        out_specs=[_full(x2d), _full(z_shape)],
        out_shape=[jax.ShapeDtypeStruct(x2d.shape, F32), z_shape],
        compiler_params=_cparams(1),
        name="out_sample_conv",
    )(u, state2d, conv_w, xa, w_out_bf, x2d)


def _ffn_sample_body(x_ref, g_ref, wg_ref, wu_ref, cg_ref, cu_ref, s0g_ref, s0u_ref, s1g_ref, s1u_ref,
                     wd_ref, gf_ref, y_ref, hg_ref, hu_ref, xn_ref, *, final_norm):
    c = pl.program_id(0)

    @pl.when(c == 0)
    def _init():
        xn_ref[...] = _rms(x_ref[...], g_ref[...]).astype(BF16)
        y_ref[...] = x_ref[...]

    def step(w_ref, cw_ref, s0_ref, s1_ref, h_ref):
        h = jnp.dot(xn_ref[...], w_ref[...], preferred_element_type=F32)
        h_ref[...] = h
        return s0_ref[...] * cw_ref[0:1, :] + s1_ref[...] * cw_ref[1:2, :] + h * cw_ref[2:3, :]

    gate = step(wg_ref, cg_ref, s0g_ref, s1g_ref, hg_ref)
    up = step(wu_ref, cu_ref, s0u_ref, s1u_ref, hu_ref)
    act = (_silu(gate) * up).astype(BF16)
    y_ref[...] += jnp.dot(act, wd_ref[...], preferred_element_type=F32)

    if final_norm:
        @pl.when(c == pl.num_programs(0) - 1)
        def _finish():
            y_ref[...] = _rms(y_ref[...], gf_ref[...])


def _ffn_sample(x2d, norm, w_up_bf, ffn_conv_w, w_down_bf, state2d, norm_final, layer, final_norm):
    m, d = x2d.shape
    dff = w_down_bf.shape[1]
    n = FFN_CHUNK
    nch = dff // n
    kw = ffn_conv_w.shape[1]
    xs = pl.BlockSpec((m, d), lambda c: (0, 0))
    col = lambda off: pl.BlockSpec((m, n), lambda c: (0, off + c))
    return pl.pallas_call(
        functools.partial(_ffn_sample_body, final_norm=final_norm),
        grid=(nch,),
        in_specs=[xs,
                  pl.BlockSpec((None, 1, d), lambda c: (layer, 0, 0)),
                  pl.BlockSpec((None, d, n), lambda c: (layer, 0, c)),
                  pl.BlockSpec((None, d, n), lambda c: (layer, 0, nch + c)),
                  pl.BlockSpec((None, kw, n), lambda c: (layer, 0, c)),
                  pl.BlockSpec((None, kw, n), lambda c: (layer, 0, nch + c)),
                  col(0), col(nch), col(2 * nch), col(3 * nch),
                  pl.BlockSpec((None, n, d), lambda c: (layer, c, 0)),
                  pl.BlockSpec((1, d), lambda c: (0, 0))],
        out_specs=[xs, col(0), col(0)],
        out_shape=[jax.ShapeDtypeStruct((m, d), F32),
                   jax.ShapeDtypeStruct((m, dff), F32),
                   jax.ShapeDtypeStruct((m, dff), F32)],
        scratch_shapes=[pltpu.VMEM((m, d), BF16)],
        compiler_params=_cparams(1),
        name="conv_ffn_sample",
    )(x2d, norm, w_up_bf, w_up_bf, ffn_conv_w, ffn_conv_w, state2d, state2d, state2d, state2d,
      w_down_bf, norm_final)


def kernel(x_prompt, x_sample, mem_prompt, cache_k, cache_v, page_table, cache_mem_k, cache_mem_v,
           state_conv, state_ffn, rel_bias, w_in, w_out, norm_mix, norm_mem, w_mem_kv,
           lambda_q1, lambda_k1, lambda_q2, lambda_k2, subln_gain, conv_w,
           norm_ffn, w_up, ffn_conv_w, w_down, norm_final):
    batch, seq, d = x_prompt.shape
    db = x_sample.shape[0]
    depth = w_in.shape[0]
    n_mem = mem_prompt.shape[1]
    n_heads = cache_k.shape[3]
    mw = n_heads * VAL_DIM
    xw = w_mem_kv.shape[-1] // 2
    n_xheads = xw // XHEAD_DIM
    dff = w_down.shape[1]
    tm = min(ROW_TILE, seq)
    assert x_sample.shape[1] == 1 and seq % tm == 0 and w_in.shape[-1] == 3 * mw + xw
    assert tm == ATTN_TILE

    w_in_bf = w_in.astype(BF16)
    w_out_bf = w_out.astype(BF16)
    w_up_bf = w_up.astype(BF16)
    w_down_bf = w_down.astype(BF16)
    w_mem_bf = w_mem_kv.astype(BF16)
    norm_mix3 = norm_mix.reshape(depth, 1, d)
    norm_ffn3 = norm_ffn.reshape(depth, 1, d)
    norm_final2 = norm_final.reshape(1, d)

    xp = x_prompt.reshape(batch * seq, d)
    xs = x_sample.reshape(db, d)

    mk_f, mv_f, mk_bf, mv_bf = _memkv(mem_prompt.reshape(batch * n_mem, d), norm_mem, w_mem_bf)

    k_rows_p = v_rows_p = None
    k_rows_s, v_rows_s = [], []
    conv_st_p, conv_st_s, ffn_st_p, ffn_st_s = [], [], [], []

    for i in range(depth):
        li = i // 2
        last = i == depth - 1
        if i % 2 == 0:
            lamp = jnp.stack([lambda_q1[li], lambda_k1[li], lambda_q2[li], lambda_k2[li]])
            gain = subln_gain[li].reshape(1, VAL_DIM)
            q_bf, k_rows_p, v_rows_p, k_bf, vt_bf, qx_bf = _proj_attn(
                xp, norm_mix3, w_in_bf, i, tm, seq, mw, xw, (depth + 1) // 2, li, (k_rows_p, v_rows_p))
            mix_bf = _attention(q_bf, k_bf, vt_bf, rel_bias, lamp, gain, batch, seq, n_heads, i)
            us = _proj_plain(xs, norm_mix3, w_in_bf, i)
            qs, ks, vs = us[:, :mw], us[:, mw:2 * mw], us[:, 2 * mw:3 * mw]
            mix_s = _paged_attention(qs, ks, vs, cache_k, cache_v, page_table, rel_bias, lamp, gain, i, li)
            xa_s = _mem_sample(us[:, 3 * mw:], cache_mem_k, cache_mem_v, i)
            xs = _out_sample_attn(mix_s, xa_s, w_out_bf, xs, i)
            k_rows_s.append(ks.reshape(db, 1, n_heads, VAL_DIM))
            v_rows_s.append(vs.reshape(db, 1, n_heads, VAL_DIM))
        else:
            mix_bf, qx_bf, cst = _proj_conv(xp, norm_mix3, w_in_bf, conv_w, i, li, tm, seq, mw, xw)
            conv_st_p.append(cst[:, SUBLANES - (conv_w.shape[1] - 1):, :])
            us = _proj_plain(xs, norm_mix3, w_in_bf, i)
            xa_s = _mem_sample(us[:, 3 * mw:], cache_mem_k, cache_mem_v, i)
            st = state_conv[li]
            xs, z_s = _out_sample_conv(us, st.reshape(db, -1), conv_w, xa_s, w_out_bf, xs, i, li, mw)
            conv_st_s.append(jnp.stack([st[:, 1, :], z_s], axis=1))

        xp = _out_proj(mix_bf, qx_bf, mk_bf, mv_bf, w_out_bf, xp, i, tm, seq)
        xp, fst = _ffn(xp, norm_ffn3, w_up_bf, ffn_conv_w, w_down_bf, norm_final2, i,
                       min(FFN_TILE, seq), seq, last)
        ffn_st_p.append(fst[:, SUBLANES - (ffn_conv_w.shape[1] - 1):, :])

        fs = state_ffn[i]
        xs, hg, hu = _ffn_sample(xs, norm_ffn3, w_up_bf, ffn_conv_w, w_down_bf, fs.reshape(db, -1),
                                 norm_final2, i, last)
        ffn_st_s.append(jnp.stack([fs[:, 1, :], jnp.concatenate([hg, hu], axis=-1)], axis=1))

    mem_shape = (depth, batch, n_mem, n_xheads, XHEAD_DIM)
    rows_shape = (-1, batch, seq, n_heads, VAL_DIM)
    return (xp.reshape(batch, seq, d), xs.reshape(db, 1, d),
            k_rows_p.reshape(rows_shape), v_rows_p.reshape(rows_shape), jnp.stack(k_rows_s), jnp.stack(v_rows_s),
            mk_f.reshape(mem_shape), mv_f.reshape(mem_shape),
            jnp.stack(conv_st_p), jnp.stack(conv_st_s), jnp.stack(ffn_st_p), jnp.stack(ffn_st_s))
```

```python
import functools
import math

import numpy as np
import jax
import jax.numpy as jnp
from jax import lax
from jax.experimental import pallas as pl
from jax.experimental.pallas import tpu as pltpu

F32 = jnp.float32
BF16 = jnp.bfloat16

HEAD_DIM = 64
VAL_DIM = 2 * HEAD_DIM
XHEAD_DIM = 128
MAX_DISTANCE = 128
EPS = 1e-6
QK_SCALE = HEAD_DIM ** -0.5
XSCALE = XHEAD_DIM ** -0.5
LOG2E = math.log2(math.e)
NEG = -1e30

LANES = 128
SUBLANES = 8
VMEM_LIMIT_MB = 56

ATTN_TILE = 512
ATTN_STRIP = 256
ATTN_HEADS = 2
ROW_TILE = 512
FFN_CHUNK = 256
FFN_TILE = 512
FFN_SPLIT = 1
PROJ_CHUNK = 512

NT_DIMS = (((1,), (1,)), ((), ()))


def _cparams(n_axes):
    return pltpu.CompilerParams(dimension_semantics=("arbitrary",) * n_axes,
                                vmem_limit_bytes=VMEM_LIMIT_MB * 1024 * 1024)


def _resident(block_shape, index_map):
    return pl.BlockSpec(block_shape, index_map, pipeline_mode=pl.Buffered(1))


def _rms(x, g):
    return x * lax.rsqrt(jnp.mean(x * x, axis=-1, keepdims=True) + EPS) * g


def _lambda_init(layer_idx):
    return 0.8 - 0.6 * math.exp(-0.3 * layer_idx)


def _lambda(lamp):
    a = jnp.sum(lamp[0:1] * lamp[1:2], axis=-1, keepdims=True)
    b = jnp.sum(lamp[2:3] * lamp[3:4], axis=-1, keepdims=True)
    return jnp.exp(a) - jnp.exp(b)


def _bucket_np(n, n_buckets):
    max_exact = n_buckets // 2
    nf = np.maximum(n, 1).astype(np.float32)
    large = max_exact + (np.log(nf / np.float32(max_exact)) / np.float32(math.log(MAX_DISTANCE / max_exact))
                         * np.float32(n_buckets - max_exact)).astype(np.int32)
    large = np.minimum(large, n_buckets - 1)
    return np.where(n < max_exact, n, large).astype(np.int32)


def _causal_conv3(z, prev, w):
    def taps(a):
        return pltpu.roll(a, 2, 0) * w[0:1, :] + pltpu.roll(a, 1, 0) * w[1:2, :] + a * w[2:3, :]

    head = taps(jnp.concatenate([prev, z[0:SUBLANES, :]], axis=0))[SUBLANES:, :]
    return jnp.concatenate([head, taps(z)[SUBLANES:, :]], axis=0)


def _silu(g):
    return g * (1.0 / (1.0 + jnp.exp(-g)))


def _memkv_body(mem_ref, g_ref, w_ref, kf_ref, vf_ref, kb_ref, vb_ref):
    xw = kf_ref.shape[-1]
    xn = _rms(mem_ref[...], g_ref[...]).astype(BF16)
    k = jnp.dot(xn, w_ref[:, :xw], preferred_element_type=F32)
    v = jnp.dot(xn, w_ref[:, xw:], preferred_element_type=F32)
    kf_ref[...] = k
    vf_ref[...] = v
    kb_ref[...] = k.astype(BF16)
    vb_ref[...] = v.astype(BF16)


def _memkv(mem2d, norm_mem, w_mem_bf):
    depth, d, two_xw = w_mem_bf.shape
    xw = two_xw // 2
    rows = mem2d.shape[0]
    out = lambda dt: jax.ShapeDtypeStruct((depth, rows, xw), dt)
    ospec = pl.BlockSpec((None, rows, xw), lambda l: (l, 0, 0))
    return pl.pallas_call(
        _memkv_body,
        grid=(depth,),
        in_specs=[pl.BlockSpec((rows, d), lambda l: (0, 0)),
                  pl.BlockSpec((None, 1, d), lambda l: (l, 0, 0)),
                  pl.BlockSpec((None, d, two_xw), lambda l: (l, 0, 0))],
        out_specs=[ospec, ospec, ospec, ospec],
        out_shape=[out(F32), out(F32), out(BF16), out(BF16)],
        compiler_params=_cparams(1),
        name="mem_kv",
    )(mem2d, norm_mem.reshape(depth, 1, d), w_mem_bf)


def _proj_attn_body(x_ref, g_ref, w_ref, *refs, first):
    if first:
        q_ref, kf_ref, vf_ref, kb_ref, vt_ref, qx_ref = refs
        for stack in (kf_ref, vf_ref):
            for other in range(1, stack.shape[0]):
                stack[other] = jnp.zeros(stack.shape[1:], F32)
        kf_ref, vf_ref = kf_ref.at[0], vf_ref.at[0]
    else:
        _, _, q_ref, kf_ref, vf_ref, kb_ref, vt_ref, qx_ref = refs
    mw = q_ref.shape[-1]
    xw = qx_ref.shape[-1]
    xn = _rms(x_ref[...], g_ref[...]).astype(BF16)

    def mm(lo, n):
        return jnp.dot(xn, w_ref[:, lo:lo + n], preferred_element_type=F32)

    n = PROJ_CHUNK
    for c in range(0, mw, n):
        q_ref[:, c:c + n] = (mm(c, n) * (QK_SCALE * LOG2E)).astype(BF16)
        k = mm(mw + c, n)
        kf_ref[:, c:c + n] = k
        kb_ref[:, c:c + n] = k.astype(BF16)
        v = mm(2 * mw + c, n)
        vf_ref[:, c:c + n] = v
        for hc in range(0, n, VAL_DIM):
            vt_ref[(c + hc) // VAL_DIM] = v[:, hc:hc + VAL_DIM].T.astype(BF16)
    qx_ref[...] = mm(3 * mw, xw).astype(BF16)


def _proj_attn(x2d, norm, w_in_bf, layer, tm, seq, mw, xw, n_slabs, slab, stacks):
    m, d = x2d.shape
    in_w = w_in_bf.shape[-1]
    tps = seq // tm
    n_heads = mw // VAL_DIM
    first = slab == 0
    row = lambda w: pl.BlockSpec((tm, w), lambda i: (i, 0))
    if first:
        stack_spec = pl.BlockSpec((n_slabs, tm, mw), lambda i: (0, i, 0))
        extra_specs, extra_args, aliases = [], (), {}
    else:
        stack_spec = pl.BlockSpec((None, tm, mw), lambda i: (slab, i, 0))
        extra_specs = [pl.BlockSpec(memory_space=pl.ANY)] * 2
        extra_args, aliases = tuple(stacks), {3: 1, 4: 2}
    stack_shape = jax.ShapeDtypeStruct((n_slabs, m, mw), F32)
    return pl.pallas_call(
        functools.partial(_proj_attn_body, first=first),
        grid=(m // tm,),
        in_specs=[row(d),
                  pl.BlockSpec((None, 1, d), lambda i: (layer, 0, 0)),
                  _resident((None, d, in_w), lambda i: (layer, 0, 0))] + extra_specs,
        out_specs=[row(mw), stack_spec, stack_spec, row(mw),
                   pl.BlockSpec((None, n_heads, None, VAL_DIM, tm), lambda i: (i // tps, 0, i % tps, 0, 0)),
                   row(xw)],
        out_shape=[jax.ShapeDtypeStruct((m, mw), BF16), stack_shape, stack_shape,
                   jax.ShapeDtypeStruct((m, mw), BF16),
                   jax.ShapeDtypeStruct((m // seq, n_heads, tps, VAL_DIM, tm), BF16),
                   jax.ShapeDtypeStruct((m, xw), BF16)],
        input_output_aliases=aliases,
        compiler_params=_cparams(1),
        name="proj_attn",
    )(x2d, norm, w_in_bf, *extra_args)


def _proj_conv_body(x_ref, g_ref, w_ref, cw_ref, mix_ref, qx_ref, st_ref, *, tiles_per_seq):
    mw = mix_ref.shape[-1]
    xw = qx_ref.shape[-1]
    tm = x_ref.shape[0]
    first = (pl.program_id(0) % tiles_per_seq) == 0
    xn = _rms(x_ref[...], g_ref[...]).astype(BF16)

    def mm(lo, n):
        return jnp.dot(xn, w_ref[:, lo:lo + n], preferred_element_type=F32)

    n = PROJ_CHUNK
    for c in range(0, mw, n):
        z = mm(mw + c, n) * mm(2 * mw + c, n)
        prev = jnp.where(first, 0.0, st_ref[:, c:c + n])
        y = _causal_conv3(z, prev, cw_ref[:, c:c + n])
        st_ref[:, c:c + n] = z[tm - SUBLANES:, :]
        mix_ref[:, c:c + n] = (mm(c, n) * y).astype(BF16)
    qx_ref[...] = mm(3 * mw, xw).astype(BF16)


def _proj_conv(x2d, norm, w_in_bf, conv_w, layer, conv_layer, tm, seq, mw, xw):
    m, d = x2d.shape
    in_w = w_in_bf.shape[-1]
    tps = seq // tm
    row = lambda w: pl.BlockSpec((tm, w), lambda i: (i, 0))
    return pl.pallas_call(
        functools.partial(_proj_conv_body, tiles_per_seq=tps),
        grid=(m // tm,),
        in_specs=[row(d),
                  pl.BlockSpec((None, 1, d), lambda i: (layer, 0, 0)),
                  _resident((None, d, in_w), lambda i: (layer, 0, 0)),
                  pl.BlockSpec((None, conv_w.shape[1], mw), lambda i: (conv_layer, 0, 0))],
        out_specs=[row(mw), row(xw),
                   pl.BlockSpec((None, SUBLANES, mw), lambda i: (i // tps, 0, 0))],
        out_shape=[jax.ShapeDtypeStruct((m, mw), BF16),
                   jax.ShapeDtypeStruct((m, xw), BF16),
                   jax.ShapeDtypeStruct((m // seq, SUBLANES, mw), F32)],
        compiler_params=_cparams(1),
        name="proj_conv",
    )(x2d, norm, w_in_bf, conv_w)


def _attn_body(tab_ref, q_ref, k_ref, vt_ref, bk_ref, lamp_ref, g_ref, o_ref,
               acc, mrun, bias2, buf_a, buf_b, bm_a, bm_b, *, lam_init):
    t = q_ref.shape[0]
    n_buckets = tab_ref.shape[0]
    hp = pl.program_id(1)
    qi = pl.program_id(2)
    nb = t // LANES
    heads = range(ATTN_HEADS)
    hcols = [slice(hh * VAL_DIM, (hh + 1) * VAL_DIM) for hh in heads]
    strips = [slice(c, c + ATTN_STRIP) for c in range(0, t, ATTN_STRIP)]

    @pl.when(qi == 0)
    def _build_bias():
        for hh in heads:
            h = hp * ATTN_HEADS + hh
            far = tab_ref[n_buckets - 1, h]

            def tile(bk):
                out = jnp.zeros(bk.shape, F32)
                for b in range(n_buckets - 1):
                    out = jnp.where(bk == b, (tab_ref[b, h] - far) * LOG2E, out)
                return jnp.where(bk < 0, NEG, out)

            d0 = tile(bk_ref[0])
            d1 = tile(bk_ref[1])
            bias2[hh] = jnp.zeros(bias2.shape[1:], F32)
            bias2[hh, 0, (nb - 1) * LANES:, 0:LANES] = d1
            for r in range(nb):
                for c in range(nb):
                    sub = (hh, 1, slice(r * LANES, (r + 1) * LANES), slice(c * LANES, (c + 1) * LANES))
                    if c < r:
                        bias2[sub] = jnp.full((LANES, LANES), NEG, F32)
                    elif c == r:
                        bias2[sub] = d0
                    elif c == r + 1:
                        bias2[sub] = d1

    lane = lax.broadcasted_iota(jnp.int32, (t, VAL_DIM), 1)
    qz = []
    for hh in heads:
        q = q_ref[:, hcols[hh]]
        zero = jnp.zeros_like(q)
        qz.append((jnp.where(lane < HEAD_DIM, q, zero), jnp.where(lane >= HEAD_DIM, q, zero)))
    ones = jnp.ones((2 * SUBLANES, t), BF16)

    acc[...] = jnp.zeros(acc.shape, F32)
    mrun[...] = jnp.full(mrun.shape, NEG, F32)

    PREV, DIAG = 0, 1

    def n_keys(cs, diag):
        return cs.stop if diag else t

    def logits_steps(j, buf, bias_idx=None):
        sbuf, bm = buf

        def step(hh, comp, cs):
            keys = n_keys(cs, bias_idx == DIAG)
            kb = k_ref[pl.ds(pl.multiple_of(j * t, t), keys), hcols[hh]]
            s = lax.dot_general(kb, qz[hh][comp][cs, :], NT_DIMS, preferred_element_type=F32)
            if bias_idx is not None:
                s = s + bias2[hh, bias_idx, 0:keys, cs]
            sbuf[hh, comp, 0:keys, cs] = s
            bm[hh, comp:comp + 1, cs] = jnp.max(s, axis=0, keepdims=True)

        return [functools.partial(step, hh, comp, cs) for hh in heads for comp in range(2) for cs in strips]

    def softmax_steps(j, buf, diag=False):
        sbuf, bm = buf

        def step(hh, comp, cs):
            keys = n_keys(cs, diag)
            vt = jnp.concatenate([vt_ref[hh, j], ones], axis=0)
            m_old = mrun[hh, comp:comp + 1, cs]
            m_new = jnp.maximum(m_old, bm[hh, comp:comp + 1, cs])
            alpha = jnp.exp2(m_old - m_new)
            mrun[hh, comp:comp + 1, cs] = m_new
            p = jnp.exp2(sbuf[hh, comp, 0:keys, cs] - m_new).astype(BF16)
            acc[hh, comp, :, cs] = (alpha * acc[hh, comp, :, cs]
                                    + jnp.dot(vt[:, 0:keys], p, preferred_element_type=F32))

        return [functools.partial(step, hh, comp, cs) for hh in heads for comp in range(2) for cs in strips]

    def run(*step_lists):
        for group in zip(*step_lists):
            for step in group:
                step()

    buf_a = (buf_a, bm_a)
    buf_b = (buf_b, bm_b)

    @pl.when(qi == 0)
    def _one_block():
        run(logits_steps(0, buf_a, DIAG))
        run(softmax_steps(0, buf_a, diag=True))

    @pl.when(qi == 1)
    def _two_blocks():
        run(logits_steps(0, buf_a, PREV))
        run(softmax_steps(0, buf_a), logits_steps(1, buf_b, DIAG))
        run(softmax_steps(1, buf_b, diag=True))

    @pl.when(qi >= 2)
    def _bias_free_blocks():
        n_pairs = (qi - 2) // 2
        run(logits_steps(0, buf_a))

        def pair(i, carry):
            j = 2 * i
            run(softmax_steps(j, buf_a), logits_steps(j + 1, buf_b))
            run(softmax_steps(j + 1, buf_b), logits_steps(j + 2, buf_a))
            return carry

        lax.fori_loop(0, n_pairs, pair, 0)

    @pl.when((qi >= 2) & (qi % 2 == 0))
    def _even_tail():
        run(softmax_steps(qi - 2, buf_a), logits_steps(qi - 1, buf_b, PREV))
        run(softmax_steps(qi - 1, buf_b), logits_steps(qi, buf_a, DIAG))
        run(softmax_steps(qi, buf_a, diag=True))

    @pl.when((qi >= 2) & (qi % 2 == 1))
    def _odd_tail():
        run(softmax_steps(qi - 3, buf_a), logits_steps(qi - 2, buf_b))
        run(softmax_steps(qi - 2, buf_b), logits_steps(qi - 1, buf_a, PREV))
        run(softmax_steps(qi - 1, buf_a), logits_steps(qi, buf_b, DIAG))
        run(softmax_steps(qi, buf_b, diag=True))

    lam = _lambda(lamp_ref[...]) + lam_init
    for hh in heads:
        inv1 = 1.0 / acc[hh, 0, VAL_DIM:VAL_DIM + 1, :]
        inv2 = 1.0 / acc[hh, 1, VAL_DIM:VAL_DIM + 1, :]
        o_t = acc[hh, 0, 0:VAL_DIM, :] * inv1 - lam * (acc[hh, 1, 0:VAL_DIM, :] * inv2)
        o = o_t.T
        y = o * lax.rsqrt(jnp.mean(o * o, axis=-1, keepdims=True) + EPS) * g_ref[...] * (1.0 - lam_init)
        o_ref[:, hcols[hh]] = y.astype(o_ref.dtype)


def _attn_bucket_tiles(n_buckets):
    kk = np.arange(LANES)[:, None]
    qq = np.arange(LANES)[None, :]
    d0 = qq - kk
    t0 = np.where(d0 >= 0, _bucket_np(np.maximum(d0, 0), n_buckets), -1)
    t1 = _bucket_np(LANES + qq - kk, n_buckets)
    return np.stack([t0, t1]).astype(np.int32)


def _attention(q_bf, k_bf, vt_bf, rel_bias, lamp, gain, batch, seq, n_heads, layer):
    t = ATTN_TILE
    nq = seq // t
    assert seq % t == 0 and t >= 2 * LANES and MAX_DISTANCE <= LANES
    m, mw = q_bf.shape
    g = ATTN_HEADS
    assert n_heads % g == 0
    bk = jnp.asarray(_attn_bucket_tiles(rel_bias.shape[0]))
    return pl.pallas_call(
        functools.partial(_attn_body, lam_init=_lambda_init(layer)),
        grid=(batch, n_heads // g, nq),
        in_specs=[pl.BlockSpec(memory_space=pltpu.SMEM),
                  pl.BlockSpec((t, g * VAL_DIM), lambda b, h, i: (b * nq + i, h)),
                  pl.BlockSpec((seq, g * VAL_DIM), lambda b, h, i: (b, h)),
                  pl.BlockSpec((None, g, nq, VAL_DIM, t), lambda b, h, i: (b, h, 0, 0, 0)),
                  pl.BlockSpec((2, LANES, LANES), lambda b, h, i: (0, 0, 0)),
                  pl.BlockSpec((4, HEAD_DIM), lambda b, h, i: (0, 0)),
                  pl.BlockSpec((1, VAL_DIM), lambda b, h, i: (0, 0))],
        out_specs=pl.BlockSpec((t, g * VAL_DIM), lambda b, h, i: (b * nq + i, h)),
        out_shape=jax.ShapeDtypeStruct((m, mw), BF16),
        scratch_shapes=[pltpu.VMEM((g, 2, VAL_DIM + 2 * SUBLANES, t), F32),
                        pltpu.VMEM((g, SUBLANES, t), F32),
                        pltpu.VMEM((g, 2, t, t), F32),
                        pltpu.VMEM((g, 2, t, t), F32),
                        pltpu.VMEM((g, 2, t, t), F32),
                        pltpu.VMEM((g, SUBLANES, t), F32), pltpu.VMEM((g, SUBLANES, t), F32)],
        compiler_params=_cparams(3),
        name="diff_attn",
    )(rel_bias, q_bf, k_bf, vt_bf, bk, lamp, gain)


def _out_body(mix_ref, qx_ref, mk_ref, mv_ref, w_ref, x_ref, o_ref):
    mw = mix_ref.shape[-1]
    xw = qx_ref.shape[-1]
    heads = [slice(c, c + XHEAD_DIM) for c in range(0, xw, XHEAD_DIM)]
    logits = [lax.dot_general(qx_ref[:, hs], mk_ref[:, hs], NT_DIMS, preferred_element_type=F32) * XSCALE
              for hs in heads]
    acc = jnp.dot(mix_ref[...], w_ref[0:mw, :], preferred_element_type=F32)
    xa = []
    for hs, s in zip(heads, logits):
        p = jnp.exp(s - jnp.max(s, axis=-1, keepdims=True))
        inv = 1.0 / jnp.sum(p, axis=-1, keepdims=True)
        xa.append((jnp.dot(p.astype(BF16), mv_ref[:, hs], preferred_element_type=F32) * inv).astype(BF16))
    acc = acc + jnp.dot(jnp.concatenate(xa, axis=-1), w_ref[mw:, :], preferred_element_type=F32)
    o_ref[...] = x_ref[...] + acc


def _out_proj(mix_bf, qx_bf, mk_bf, mv_bf, w_out_bf, x2d, layer, tm, seq):
    m, d = x2d.shape
    mw = mix_bf.shape[-1]
    xw = qx_bf.shape[-1]
    n_mem = mk_bf.shape[1] // (m // seq)
    tps = seq // tm
    row = lambda w: pl.BlockSpec((tm, w), lambda i: (i, 0))
    mem = pl.BlockSpec((None, n_mem, xw), lambda i: (layer, i // tps, 0))
    return pl.pallas_call(
        _out_body,
        grid=(m // tm,),
        in_specs=[row(mw), row(xw), mem, mem,
                  _resident((None, mw + xw, d), lambda i: (layer, 0, 0)),
                  row(d)],
        out_specs=row(d),
        out_shape=jax.ShapeDtypeStruct((m, d), F32),
        compiler_params=_cparams(1),
        name="out_proj",
    )(mix_bf, qx_bf, mk_bf, mv_bf, w_out_bf, x2d)


def _ffn_body(x_ref, g_ref, wu_ref, cw_ref, wd_ref, gf_ref, o_ref, st_ref, act_ref,
              *, tiles_per_seq, final_norm):
    tm = x_ref.shape[0] // FFN_SPLIT
    dff = wd_ref.shape[0]
    first = (pl.program_id(0) % tiles_per_seq) == 0
    n = FFN_CHUNK
    for part in range(FFN_SPLIT):
        rows = slice(part * tm, (part + 1) * tm)
        x = x_ref[rows, :]
        xn = _rms(x, g_ref[...]).astype(BF16)

        def conv_cols(lo):
            h = jnp.dot(xn, wu_ref[:, lo:lo + n], preferred_element_type=F32)
            prev = st_ref[:, lo:lo + n]
            if part == 0:
                prev = jnp.where(first, 0.0, prev)
            st_ref[:, lo:lo + n] = h[tm - SUBLANES:, :]
            return _causal_conv3(h, prev, cw_ref[:, lo:lo + n])

        for c in range(0, dff, n):
            act_ref[part, :, c:c + n] = (_silu(conv_cols(c)) * conv_cols(dff + c)).astype(BF16)
        y = x + jnp.dot(act_ref[part], wd_ref[...], preferred_element_type=F32)
        if final_norm:
            y = _rms(y, gf_ref[...])
        o_ref[rows, :] = y


def _ffn(x2d, norm, w_up_bf, ffn_conv_w, w_down_bf, norm_final, layer, tm, seq, final_norm):
    m, d = x2d.shape
    dff = w_down_bf.shape[1]
    assert dff % FFN_CHUNK == 0
    tps = seq // tm
    row = pl.BlockSpec((tm, d), lambda i: (i, 0))
    return pl.pallas_call(
        functools.partial(_ffn_body, tiles_per_seq=tps, final_norm=final_norm),
        grid=(m // tm,),
        in_specs=[row,
                  pl.BlockSpec((None, 1, d), lambda i: (layer, 0, 0)),
                  _resident((None, d, 2 * dff), lambda i: (layer, 0, 0)),
                  pl.BlockSpec((None, ffn_conv_w.shape[1], 2 * dff), lambda i: (layer, 0, 0)),
                  _resident((None, dff, d), lambda i: (layer, 0, 0)),
                  pl.BlockSpec((1, d), lambda i: (0, 0))],
        out_specs=[row, pl.BlockSpec((None, SUBLANES, 2 * dff), lambda i: (i // tps, 0, 0))],
        out_shape=[jax.ShapeDtypeStruct((m, d), F32),
                   jax.ShapeDtypeStruct((m // seq, SUBLANES, 2 * dff), F32)],
        scratch_shapes=[pltpu.VMEM((FFN_SPLIT, tm // FFN_SPLIT, dff), BF16)],
        compiler_params=_cparams(1),
        name="conv_ffn",
    )(x2d, norm, w_up_bf, ffn_conv_w, w_down_bf, norm_final)


def _proj_plain_body(x_ref, g_ref, w_ref, u_ref):
    xn = _rms(x_ref[...], g_ref[...]).astype(BF16)
    u_ref[...] = jnp.dot(xn, w_ref[...], preferred_element_type=F32)


def _proj_plain(x2d, norm, w_in_bf, layer):
    m, d = x2d.shape
    in_w = w_in_bf.shape[-1]
    n = PROJ_CHUNK
    return pl.pallas_call(
        _proj_plain_body,
        grid=(in_w // n,),
        in_specs=[pl.BlockSpec((m, d), lambda c: (0, 0)),
                  pl.BlockSpec((None, 1, d), lambda c: (layer, 0, 0)),
                  pl.BlockSpec((None, d, n), lambda c: (layer, 0, c))],
        out_specs=pl.BlockSpec((m, n), lambda c: (0, c)),
        out_shape=jax.ShapeDtypeStruct((m, in_w), F32),
        compiler_params=_cparams(1),
        name="proj_sample",
    )(x2d, norm, w_in_bf)


def _paged_attn_body(pt_ref, q_ref, kn_ref, vn_ref, bks_ref, tab_ref, lamp_ref, g_ref, *rest,
                     n_pages, lam_init):
    k_refs = rest[:n_pages]
    v_refs = rest[n_pages:2 * n_pages]
    o_ref, s_ref, bias_ref = rest[2 * n_pages:]
    rows = k_refs[0].shape[0]
    n_heads = kn_ref.shape[0]
    past = n_pages * rows
    n_buckets = tab_ref.shape[1]

    @pl.when(pl.program_id(0) == 0)
    def _build_bias():
        bk = bks_ref[...]
        out = jnp.zeros(bk.shape, F32)
        for b in range(n_buckets):
            out = jnp.where(bk == b, tab_ref[:, b:b + 1], out)
        bias_ref[...] = jnp.where(bk < 0, NEG, out)

    q = q_ref[...]
    for r in range(n_pages):
        s_ref[:, r * rows:(r + 1) * rows] = lax.dot_general(
            q, k_refs[r][...], NT_DIMS, preferred_element_type=F32)
    pad = jnp.zeros((LANES - n_heads, VAL_DIM), F32)
    s_ref[:, past:] = lax.dot_general(q, jnp.concatenate([kn_ref[...], pad], axis=0), NT_DIMS,
                                      preferred_element_type=F32)

    s = s_ref[...] + bias_ref[...]
    p = jnp.exp(s - jnp.max(s, axis=-1, keepdims=True))
    pn = p * (1.0 / jnp.sum(p, axis=-1, keepdims=True))
    pv = jnp.dot(pn[:, past:], jnp.concatenate([vn_ref[...], pad], axis=0), preferred_element_type=F32)
    for r in range(n_pages):
        pv = pv + jnp.dot(pn[:, r * rows:(r + 1) * rows], v_refs[r][...], preferred_element_type=F32)
    lam = _lambda(lamp_ref[...]) + lam_init
    o = pv[0:n_heads, :] - lam * pv[n_heads:, :]
    o_ref[...] = o * lax.rsqrt(jnp.mean(o * o, axis=-1, keepdims=True) + EPS) * g_ref[...] * (1.0 - lam_init)


def _paged_attention(q, k_new, v_new, cache_k, cache_v, page_table, rel_bias, lamp, gain, layer, attn_layer):
    db, width = q.shape
    n_layers, n_phys, page, n_heads, vd = cache_k.shape
    n_pages = page_table.shape[1]
    n_buckets = rel_bias.shape[0]
    rows = page * n_heads
    past = n_pages * rows
    assert vd == VAL_DIM == LANES and n_heads <= LANES
    ck = cache_k.reshape(n_layers, n_phys, rows, vd)
    cv = cache_v.reshape(n_layers, n_phys, rows, vd)
    q3 = q.reshape(db, 1, n_heads, vd) * QK_SCALE
    comp = (np.arange(vd) // HEAD_DIM)[None, None, None, :] == np.arange(2)[None, :, None, None]
    qrows = jnp.where(comp, q3, 0.0).reshape(db, 2 * n_heads, vd)
    col = np.arange(past + LANES)
    key = np.minimum(col // n_heads, past // n_heads)
    valid = (col < past + n_heads)[None, :] & ((col % n_heads)[None, :]
                                               == (np.arange(2 * n_heads) % n_heads)[:, None])
    bucket = _bucket_np(past // n_heads - key, n_buckets)
    bks = jnp.asarray(np.where(valid, bucket[None, :], -1).astype(np.int32))
    tab = jnp.tile(rel_bias.T, (2, 1))
    head_rows = lambda a: a.reshape(db, n_heads, vd)
    hspec = pl.BlockSpec((None, n_heads, vd), lambda b, pt: (b, 0, 0))

    def page_spec(r):
        return pl.BlockSpec((None, None, rows, vd), lambda b, pt: (attn_layer, pt[b, r], 0, 0))

    const = lambda shape: pl.BlockSpec(shape, lambda b, pt: (0,) * len(shape))
    grid_spec = pltpu.PrefetchScalarGridSpec(
        num_scalar_prefetch=1,
        grid=(db,),
        in_specs=[pl.BlockSpec((None, 2 * n_heads, vd), lambda b, pt: (b, 0, 0)), hspec, hspec,
                  const(bks.shape), const(tab.shape), const((4, HEAD_DIM)), const((1, VAL_DIM))]
                 + [page_spec(r) for r in range(n_pages)] * 2,
        out_specs=hspec,
        scratch_shapes=[pltpu.VMEM(bks.shape, F32), pltpu.VMEM(bks.shape, F32)],
    )
    out = pl.pallas_call(
        functools.partial(_paged_attn_body, n_pages=n_pages, lam_init=_lambda_init(layer)),
        grid_spec=grid_spec,
        out_shape=jax.ShapeDtypeStruct((db, n_heads, vd), F32),
        compiler_params=_cparams(1),
        name="paged_attn",
    )(page_table, qrows, head_rows(k_new), head_rows(v_new), bks, tab, lamp, gain,
      *([ck] * n_pages), *([cv] * n_pages))
    return out.reshape(db, width)


def _mem_sample_body(qx_ref, mk_ref, mv_ref, o_ref):
    g, n_mem, n_xheads, xd = mk_ref.shape
    rows = n_mem * n_xheads
    col = lax.broadcasted_iota(jnp.int32, (SUBLANES, rows), 1)
    row = lax.broadcasted_iota(jnp.int32, (SUBLANES, rows), 0)
    own = col % n_xheads == row
    logits = [jnp.where(own, lax.dot_general(qx_ref[b], mk_ref[b].reshape(rows, xd), NT_DIMS,
                                             preferred_element_type=F32) * XSCALE, NEG)
              for b in range(g)]
    weights = []
    for s in logits:
        p = jnp.exp(s - jnp.max(s, axis=-1, keepdims=True))
        weights.append(p * (1.0 / jnp.sum(p, axis=-1, keepdims=True)))
    for b, pn in enumerate(weights):
        o_ref[b] = jnp.dot(pn, mv_ref[b].reshape(rows, xd), preferred_element_type=F32)


def _mem_sample(qx, cache_mem_k, cache_mem_v, layer):
    db, xw = qx.shape
    depth, _, n_mem, n_xheads, xd = cache_mem_k.shape
    assert xd == XHEAD_DIM == LANES and n_xheads <= SUBLANES
    g = SUBLANES
    qx8 = jnp.pad(qx.reshape(db, n_xheads, xd), ((0, 0), (0, SUBLANES - n_xheads), (0, 0)))
    mem = pl.BlockSpec((None, g, n_mem, n_xheads, xd), lambda i: (layer, i, 0, 0, 0))
    vec = pl.BlockSpec((g, SUBLANES, xd), lambda i: (i, 0, 0))
    out = pl.pallas_call(
        _mem_sample_body,
        grid=(db // g,),
        in_specs=[vec, mem, mem],
        out_specs=vec,
        out_shape=jax.ShapeDtypeStruct((db, SUBLANES, xd), F32),
        compiler_params=_cparams(1),
        name="mem_attn_sample",
    )(qx8, cache_mem_k, cache_mem_v)
    return out[:, :n_xheads, :].reshape(db, xw)


def _out_sample_attn_body(mix_ref, xa_ref, w_ref, x_ref, y_ref):
    mw = mix_ref.shape[-1]
    acc = jnp.dot(mix_ref[...].astype(BF16), w_ref[0:mw, :], preferred_element_type=F32)
    acc = acc + jnp.dot(xa_ref[...].astype(BF16), w_ref[mw:, :], preferred_element_type=F32)
    y_ref[...] = x_ref[...] + acc


def _out_sample_conv_body(u_ref, st_ref, cw_ref, xa_ref, w_ref, x_ref, y_ref, z_ref):
    mw = z_ref.shape[-1]
    z = u_ref[:, mw:2 * mw] * u_ref[:, 2 * mw:3 * mw]
    conv = st_ref[:, 0:mw] * cw_ref[0:1, :] + st_ref[:, mw:] * cw_ref[1:2, :] + z * cw_ref[2:3, :]
    mix = u_ref[:, 0:mw] * conv
    z_ref[...] = z
    acc = jnp.dot(mix.astype(BF16), w_ref[0:mw, :], preferred_element_type=F32)
    acc = acc + jnp.dot(xa_ref[...].astype(BF16), w_ref[mw:, :], preferred_element_type=F32)
    y_ref[...] = x_ref[...] + acc


def _full(a):
    return pl.BlockSpec(a.shape, lambda i: (0,) * a.ndim)


def _out_sample_attn(mix, xa, w_out_bf, x2d, layer):
    w_spec = pl.BlockSpec((None,) + w_out_bf.shape[1:], lambda i: (layer, 0, 0))
    return pl.pallas_call(
        _out_sample_attn_body,
        grid=(1,),
        in_specs=[_full(mix), _full(xa), w_spec, _full(x2d)],
        out_specs=_full(x2d),
        out_shape=jax.ShapeDtypeStruct(x2d.shape, F32),
        compiler_params=_cparams(1),
        name="out_sample_attn",
    )(mix, xa, w_out_bf, x2d)


def _out_sample_conv(u, state2d, conv_w, xa, w_out_bf, x2d, layer, conv_layer, mw):
    db = u.shape[0]
    w_spec = pl.BlockSpec((None,) + w_out_bf.shape[1:], lambda i: (layer, 0, 0))
    cw_spec = pl.BlockSpec((None,) + conv_w.shape[1:], lambda i: (conv_layer, 0, 0))
    z_shape = jax.ShapeDtypeStruct((db, mw), F32)
    return pl.pallas_call(
        _out_sample_conv_body,
        grid=(1,),
        in_specs=[_full(u), _full(state2d), cw_spec, _full(xa), w_spec, _full(x2d)],
        out_specs=[_full(x2d), _full(z_shape)],
        out_shape=[jax.ShapeDtypeStruct(x2d.shape, F32), z_shape],
        compiler_params=_cparams(1),
        name="out_sample_conv",
    )(u, state2d, conv_w, xa, w_out_bf, x2d)


def _ffn_sample_body(x_ref, g_ref, wg_ref, wu_ref, cg_ref, cu_ref, sg_ref, su_ref,
                     wd_ref, gf_ref, y_ref, hg_ref, hu_ref, pg_ref, pu_ref, xn_ref, *, final_norm):
    c = pl.program_id(0)

    @pl.when(c == 0)
    def _init():
        xn_ref[...] = _rms(x_ref[...], g_ref[...]).astype(BF16)
        y_ref[...] = x_ref[...]

    def step(w_ref, cw_ref, s_ref, h_ref, p_ref):
        h = jnp.dot(xn_ref[...], w_ref[...], preferred_element_type=F32)
        s0, s1 = s_ref[:, 0, :], s_ref[:, 1, :]
        h_ref[...] = h
        p_ref[...] = s1
        return s0 * cw_ref[0:1, :] + s1 * cw_ref[1:2, :] + h * cw_ref[2:3, :]

    gate = step(wg_ref, cg_ref, sg_ref, hg_ref, pg_ref)
    up = step(wu_ref, cu_ref, su_ref, hu_ref, pu_ref)
    act = (_silu(gate) * up).astype(BF16)
    y_ref[...] += jnp.dot(act, wd_ref[...], preferred_element_type=F32)

    if final_norm:
        @pl.when(c == pl.num_programs(0) - 1)
        def _finish():
            y_ref[...] = _rms(y_ref[...], gf_ref[...])


def _ffn_sample(x2d, norm, w_up_bf, ffn_conv_w, w_down_bf, state_ffn, norm_final, layer, final_norm):
    m, d = x2d.shape
    dff = w_down_bf.shape[1]
    n = FFN_CHUNK
    nch = dff // n
    kw = ffn_conv_w.shape[1]
    xs = pl.BlockSpec((m, d), lambda c: (0, 0))
    col = pl.BlockSpec((m, n), lambda c: (0, c))
    state = lambda off: pl.BlockSpec((None, m, kw - 1, n), lambda c: (layer, 0, 0, off + c))
    half = jax.ShapeDtypeStruct((m, dff), F32)
    return pl.pallas_call(
        functools.partial(_ffn_sample_body, final_norm=final_norm),
        grid=(nch,),
        in_specs=[xs,
                  pl.BlockSpec((None, 1, d), lambda c: (layer, 0, 0)),
                  pl.BlockSpec((None, d, n), lambda c: (layer, 0, c)),
                  pl.BlockSpec((None, d, n), lambda c: (layer, 0, nch + c)),
                  pl.BlockSpec((None, kw, n), lambda c: (layer, 0, c)),
                  pl.BlockSpec((None, kw, n), lambda c: (layer, 0, nch + c)),
                  state(0), state(nch),
                  pl.BlockSpec((None, n, d), lambda c: (layer, c, 0)),
                  pl.BlockSpec((1, d), lambda c: (0, 0))],
        out_specs=[xs, col, col, col, col],
        out_shape=[jax.ShapeDtypeStruct((m, d), F32), half, half, half, half],
        scratch_shapes=[pltpu.VMEM((m, d), BF16)],
        compiler_params=_cparams(1),
        name="conv_ffn_sample",
    )(x2d, norm, w_up_bf, w_up_bf, ffn_conv_w, ffn_conv_w, state_ffn, state_ffn, w_down_bf, norm_final)


def kernel(x_prompt, x_sample, mem_prompt, cache_k, cache_v, page_table, cache_mem_k, cache_mem_v,
           state_conv, state_ffn, rel_bias, w_in, w_out, norm_mix, norm_mem, w_mem_kv,
           lambda_q1, lambda_k1, lambda_q2, lambda_k2, subln_gain, conv_w,
           norm_ffn, w_up, ffn_conv_w, w_down, norm_final):
    batch, seq, d = x_prompt.shape
    db = x_sample.shape[0]
    depth = w_in.shape[0]
    n_mem = mem_prompt.shape[1]
    n_heads = cache_k.shape[3]
    mw = n_heads * VAL_DIM
    xw = w_mem_kv.shape[-1] // 2
    n_xheads = xw // XHEAD_DIM
    dff = w_down.shape[1]
    tm = min(ROW_TILE, seq)
    assert x_sample.shape[1] == 1 and seq % tm == 0 and w_in.shape[-1] == 3 * mw + xw
    assert tm == ATTN_TILE

    w_in_bf = w_in.astype(BF16)
    w_out_bf = w_out.astype(BF16)
    w_up_bf = w_up.astype(BF16)
    w_down_bf = w_down.astype(BF16)
    w_mem_bf = w_mem_kv.astype(BF16)
    norm_mix3 = norm_mix.reshape(depth, 1, d)
    norm_ffn3 = norm_ffn.reshape(depth, 1, d)
    norm_final2 = norm_final.reshape(1, d)

    xp = x_prompt.reshape(batch * seq, d)
    xs = x_sample.reshape(db, d)

    mk_f, mv_f, mk_bf, mv_bf = _memkv(mem_prompt.reshape(batch * n_mem, d), norm_mem, w_mem_bf)

    k_rows_p = v_rows_p = None
    k_rows_s, v_rows_s = [], []
    conv_st_p, conv_st_s, ffn_st_p, ffn_st_s = [], [], [], []

    for i in range(depth):
        li = i // 2
        last = i == depth - 1
        if i % 2 == 0:
            lamp = jnp.stack([lambda_q1[li], lambda_k1[li], lambda_q2[li], lambda_k2[li]])
            gain = subln_gain[li].reshape(1, VAL_DIM)
            q_bf, k_rows_p, v_rows_p, k_bf, vt_bf, qx_bf = _proj_attn(
                xp, norm_mix3, w_in_bf, i, tm, seq, mw, xw, (depth + 1) // 2, li, (k_rows_p, v_rows_p))
            mix_bf = _attention(q_bf, k_bf, vt_bf, rel_bias, lamp, gain, batch, seq, n_heads, i)
            us = _proj_plain(xs, norm_mix3, w_in_bf, i)
            qs, ks, vs = us[:, :mw], us[:, mw:2 * mw], us[:, 2 * mw:3 * mw]
            mix_s = _paged_attention(qs, ks, vs, cache_k, cache_v, page_table, rel_bias, lamp, gain, i, li)
            xa_s = _mem_sample(us[:, 3 * mw:], cache_mem_k, cache_mem_v, i)
            xs = _out_sample_attn(mix_s, xa_s, w_out_bf, xs, i)
            k_rows_s.append(ks.reshape(db, 1, n_heads, VAL_DIM))
            v_rows_s.append(vs.reshape(db, 1, n_heads, VAL_DIM))
        else:
            mix_bf, qx_bf, cst = _proj_conv(xp, norm_mix3, w_in_bf, conv_w, i, li, tm, seq, mw, xw)
            conv_st_p.append(cst[:, SUBLANES - (conv_w.shape[1] - 1):, :])
            us = _proj_plain(xs, norm_mix3, w_in_bf, i)
            xa_s = _mem_sample(us[:, 3 * mw:], cache_mem_k, cache_mem_v, i)
            st = state_conv[li]
            xs, z_s = _out_sample_conv(us, st.reshape(db, -1), conv_w, xa_s, w_out_bf, xs, i, li, mw)
            conv_st_s.append(jnp.stack([st[:, 1, :], z_s], axis=1))

        xp = _out_proj(mix_bf, qx_bf, mk_bf, mv_bf, w_out_bf, xp, i, tm, seq)
        xp, fst = _ffn(xp, norm_ffn3, w_up_bf, ffn_conv_w, w_down_bf, norm_final2, i,
                       min(FFN_TILE, seq), seq, last)
        ffn_st_p.append(fst[:, SUBLANES - (ffn_conv_w.shape[1] - 1):, :])

        xs, hg, hu, pg, pu = _ffn_sample(xs, norm_ffn3, w_up_bf, ffn_conv_w, w_down_bf, state_ffn,
                                         norm_final2, i, last)
        ffn_st_s.append(jnp.stack([jnp.concatenate([pg, pu], axis=-1),
                                   jnp.concatenate([hg, hu], axis=-1)], axis=1))

    mem_shape = (depth, batch, n_mem, n_xheads, XHEAD_DIM)
    rows_shape = (-1, batch, seq, n_heads, VAL_DIM)
    return (xp.reshape(batch, seq, d), xs.reshape(db, 1, d),
            k_rows_p.reshape(rows_shape), v_rows_p.reshape(rows_shape), jnp.stack(k_rows_s), jnp.stack(v_rows_s),
            mk_f.reshape(mem_shape), mv_f.reshape(mem_shape),
            jnp.stack(conv_st_p), jnp.stack(conv_st_s), jnp.stack(ffn_st_p), jnp.stack(ffn_st_s))
```
